```python
import math
import jax
import jax.numpy as jnp
from jax import lax
import numpy as np

D_MODEL = 1024
BATCH = 2
SEQ = 8192
DEPTH = 4
DEC_BATCH = 32
DEC_SEQ = 32
PAST_LEN = 1024

CHUNK = 64
Q_BLOCK = 128
HEAD_DIM = 64
ROT_DIM = HEAD_DIM // 4
ROPE_THETA = 500000.0
A_HEADS = D_MODEL // 128
B_HEADS = D_MODEL // 128
A_WIDTH = A_HEADS * HEAD_DIM
B_WIDTH = B_HEADS * HEAD_DIM
IDX_HEADS = 8
IDX_DIM = 64
IDX_TOPK_MAX = 256
C_HEADS = D_MODEL // 128
C_QK_DIM = 64
C_V_DIM = 2 * C_QK_DIM
C_WIDTH = C_HEADS * C_V_DIM
N_EVEN = (DEPTH + 1) // 2
N_ODD = DEPTH // 2
EVEN_SPLITS = (A_WIDTH, A_WIDTH, A_WIDTH, A_WIDTH, A_HEADS,
               B_WIDTH, B_WIDTH, B_WIDTH, B_WIDTH, IDX_HEADS * IDX_DIM, IDX_DIM, IDX_HEADS)
EVEN_IN = sum(EVEN_SPLITS)
EVEN_MIX = A_WIDTH + B_WIDTH
ODD_IN = 4 * C_WIDTH
FORGET_BIAS = 2.0
NORM_EPS = 1e-6
NEG_INF = -1e30

kernel_name = 'hybrid_fox_dsa_diff_stream_step'


def rms_norm(x, g):
    xf = x.astype(jnp.float32)
    y = xf * lax.rsqrt(jnp.mean(xf * xf, axis=-1, keepdims=True) + NORM_EPS)
    return (y * g.astype(jnp.float32)).astype(x.dtype)


def partial_rope(x, pos):
    half = ROT_DIM // 2
    inv = ROPE_THETA ** (-jnp.arange(half, dtype=jnp.float32) * 2.0 / ROT_DIM)
    ang = pos.astype(jnp.float32)[:, None] * inv[None, :]
    ang = ang.reshape(ang.shape[0], *([1] * (x.ndim - 3)), half)
    cos, sin = jnp.cos(ang), jnp.sin(ang)
    xr = x[..., :ROT_DIM].astype(jnp.float32)
    x1, x2 = xr[..., :half], xr[..., half:]
    rot = jnp.concatenate([x1 * cos - x2 * sin, x2 * cos + x1 * sin], axis=-1).astype(x.dtype)
    return jnp.concatenate([rot, x[..., ROT_DIM:]], axis=-1)


def with_past(cache, new):
    return new if cache is None else jnp.concatenate([cache, new], axis=1)


def sweep_query_blocks(block_fn, n_q):
    if n_q <= Q_BLOCK:
        return block_fn(0, n_q)
    n_blocks = n_q // Q_BLOCK
    out = lax.map(lambda i: block_fn(i * Q_BLOCK, Q_BLOCK), jnp.arange(n_blocks))
    out = jnp.moveaxis(out, 0, 1)
    return out.reshape(out.shape[0], n_q, *out.shape[3:])


def fox_attention(q, k_all, v_all, logf_all, past_len):
    n_q, n_k = q.shape[1], k_all.shape[1]
    cum = jnp.cumsum(logf_all.astype(jnp.float32), axis=1)
    cum_k = jnp.swapaxes(cum, 1, 2)
    cum_q = cum[:, past_len:]
    k_pos = jnp.arange(n_k)
    scale = HEAD_DIM ** -0.5

    def block(start, size):
        qb = lax.dynamic_slice_in_dim(q, start, size, axis=1)
        cq = jnp.swapaxes(lax.dynamic_slice_in_dim(cum_q, start, size, axis=1), 1, 2)
        q_pos = past_len + start + jnp.arange(size)
        s = jnp.einsum('bqhd,bkhd->bhqk', qb, k_all).astype(jnp.float32) * scale
        s = s + cq[..., None] - cum_k[:, :, None, :]
        s = jnp.where(k_pos[None, :] <= q_pos[:, None], s, NEG_INF)
        p = jax.nn.softmax(s, axis=-1).astype(v_all.dtype)
        return jnp.einsum('bhqk,bkhd->bqhd', p, v_all)

    return sweep_query_blocks(block, n_q)


def gather_rows(rows, sel):
    return jax.vmap(lambda r, i: r[i])(rows, sel)


def dsa_attention(q, k_all, v_all, qi, ki_all, wi, past_len):
    n_q, n_k = q.shape[1], k_all.shape[1]
    topk = min(IDX_TOPK_MAX, n_k // 4)
    k_chunk = jnp.arange(n_k) // CHUNK
    scale = HEAD_DIM ** -0.5

    def block(start, size):
        qb = lax.dynamic_slice_in_dim(q, start, size, axis=1)
        qib = lax.dynamic_slice_in_dim(qi, start, size, axis=1)
        wib = lax.dynamic_slice_in_dim(wi, start, size, axis=1)
        q_chunk = (past_len + start + jnp.arange(size)) // CHUNK
        dots = jnp.einsum('bqhe,bke->bqhk', qib, ki_all).astype(jnp.float32) * (IDX_DIM ** -0.5)
        score = jnp.einsum('bqh,bqhk->bqk', wib.astype(jnp.float32) * (IDX_HEADS ** -0.5), jax.nn.relu(dots))
        score = jnp.where((k_chunk[None, :] <= q_chunk[:, None])[None], score, NEG_INF)
        _, sel = lax.top_k(score, topk)
        valid = (sel // CHUNK) <= q_chunk[None, :, None]
        k_sel = gather_rows(k_all, sel)
        v_sel = gather_rows(v_all, sel)
        s = jnp.einsum('bqhd,bqkhd->bhqk', qb, k_sel).astype(jnp.float32) * scale
        s = jnp.where(valid[:, None], s, NEG_INF)
        p = jax.nn.softmax(s, axis=-1).astype(v_all.dtype)
        return jnp.einsum('bhqk,bqkhd->bqhd', p, v_sel)

    return sweep_query_blocks(block, n_q)


def diff_attention(q, k_all, v_all, lam, past_len):
    n_q, n_k = q.shape[1], k_all.shape[1]
    k_chunk = jnp.arange(n_k) // CHUNK
    scale = C_QK_DIM ** -0.5

    def block(start, size):
        qb = lax.dynamic_slice_in_dim(q, start, size, axis=1)
        q_chunk = (past_len + start + jnp.arange(size)) // CHUNK
        s = jnp.einsum('bqhcd,bkhcd->bhcqk', qb, k_all).astype(jnp.float32) * scale
        s = jnp.where(k_chunk[None, :] <= q_chunk[:, None], s, NEG_INF)
        p = jax.nn.softmax(s, axis=-1)
        a = (p[:, :, 0] - lam * p[:, :, 1]).astype(v_all.dtype)
        return jnp.einsum('bhqk,bkhe->bqhe', a, v_all)

    return sweep_query_blocks(block, n_q)


def even_mixer(h, w_in, w_out, b_forget, past_len, cache):
    n_b, n_l, _ = h.shape
    pos = past_len + jnp.arange(n_l)
    cuts = [int(c) for c in np.cumsum(EVEN_SPLITS)[:-1]]
    aq, ak, av, ag, af, bq, bk, bv, bg, qi, ki, wi = jnp.split(h @ w_in, cuts, axis=-1)
    c_ak, c_av, c_af, c_bk, c_bv, c_bki = cache if cache is not None else (None,) * 6
    aq = aq.reshape(n_b, n_l, A_HEADS, HEAD_DIM)
    ak = ak.reshape(n_b, n_l, A_HEADS, HEAD_DIM)
    av = av.reshape(n_b, n_l, A_HEADS, HEAD_DIM)
    logf = jax.nn.log_sigmoid(af.astype(jnp.float32) + b_forget.astype(jnp.float32))
    a_out = fox_attention(aq, with_past(c_ak, ak), with_past(c_av, av), with_past(c_af, logf), past_len)
    bq = partial_rope(bq.reshape(n_b, n_l, B_HEADS, HEAD_DIM), pos)
    bk = partial_rope(bk.reshape(n_b, n_l, B_HEADS, HEAD_DIM), pos)
    bv = bv.reshape(n_b, n_l, B_HEADS, HEAD_DIM)
    qi = partial_rope(qi.reshape(n_b, n_l, IDX_HEADS, IDX_DIM), pos)
    ki = partial_rope(ki, pos)
    b_out = dsa_attention(bq, with_past(c_bk, bk), with_past(c_bv, bv), qi, with_past(c_bki, ki), wi, past_len)
    mixed = jnp.concatenate([a_out.reshape(n_b, n_l, A_WIDTH) * jax.nn.silu(ag),
                             b_out.reshape(n_b, n_l, B_WIDTH) * jax.nn.silu(bg)], axis=-1)
    return mixed @ w_out, (ak, av, logf, bk, bv, ki)


def odd_mixer(h, w_in, lam_params, head_gain, w_out, lam_init, past_len, cache):
    n_b, n_l, _ = h.shape
    pos = past_len + jnp.arange(n_l)
    q, k, v, g = jnp.split(h @ w_in, 4, axis=-1)
    c_ck, c_cv = cache if cache is not None else (None, None)
    q = partial_rope(q.reshape(n_b, n_l, C_HEADS, 2, C_QK_DIM), pos)
    k = partial_rope(k.reshape(n_b, n_l, C_HEADS, 2, C_QK_DIM), pos)
    v = v.reshape(n_b, n_l, C_HEADS, C_V_DIM)
    lp = lam_params.astype(jnp.float32)
    lam = jnp.exp(jnp.sum(lp[0] * lp[1])) - jnp.exp(jnp.sum(lp[2] * lp[3])) + lam_init
    k_flat = k.reshape(n_b, n_l, C_HEADS, 2 * C_QK_DIM)
    k_all = with_past(c_ck, k_flat)
    k_all = k_all.reshape(n_b, k_all.shape[1], C_HEADS, 2, C_QK_DIM)
    o = diff_attention(q, k_all, with_past(c_cv, v), lam, past_len)
    o = rms_norm(o, head_gain) * (1.0 - lam_init)
    mixed = o.reshape(n_b, n_l, C_WIDTH) * jax.nn.silu(g)
    return mixed @ w_out, (k_flat, v)


def setup_inputs(seed: int = 0) -> dict:
    key = jax.random.key(seed)
    ks = jax.random.split(key, 20)

    def nrm(k, shape, s):
        return jax.random.normal(k, shape, jnp.float32) * s

    return {
        'x_prompt': nrm(ks[0], (BATCH, SEQ, D_MODEL), 1.0),
        'x_sample': nrm(ks[1], (DEC_BATCH, DEC_SEQ, D_MODEL), 1.0),
        'cache_a_k': nrm(ks[2], (N_EVEN, DEC_BATCH, PAST_LEN, A_HEADS, HEAD_DIM), 1.0),
        'cache_a_v': nrm(ks[3], (N_EVEN, DEC_BATCH, PAST_LEN, A_HEADS, HEAD_DIM), 1.0),
        'cache_a_logf': jax.nn.log_sigmoid(FORGET_BIAS + nrm(ks[4], (N_EVEN, DEC_BATCH, PAST_LEN, A_HEADS), 1.0)),
        'cache_b_k': nrm(ks[5], (N_EVEN, DEC_BATCH, PAST_LEN, B_HEADS, HEAD_DIM), 1.0),
        'cache_b_v': nrm(ks[6], (N_EVEN, DEC_BATCH, PAST_LEN, B_HEADS, HEAD_DIM), 1.0),
        'cache_b_kidx': nrm(ks[7], (N_EVEN, DEC_BATCH, PAST_LEN, IDX_DIM), 1.0),
        'cache_c_k': nrm(ks[8], (N_ODD, DEC_BATCH, PAST_LEN, C_HEADS, 2 * C_QK_DIM), 1.0),
        'cache_c_v': nrm(ks[9], (N_ODD, DEC_BATCH, PAST_LEN, C_HEADS, C_V_DIM), 1.0),
        'norm_gain': 1.0 + nrm(ks[10], (DEPTH, D_MODEL), 0.02),
        'final_gain': 1.0 + nrm(ks[11], (D_MODEL,), 0.02),
        'w_in_even': nrm(ks[12], (N_EVEN, D_MODEL, EVEN_IN), D_MODEL ** -0.5),
        'b_forget': FORGET_BIAS + nrm(ks[13], (N_EVEN, A_HEADS), 0.5),
        'w_out_even': nrm(ks[14], (N_EVEN, EVEN_MIX, D_MODEL), EVEN_MIX ** -0.5),
        'w_in_odd': nrm(ks[15], (N_ODD, D_MODEL, ODD_IN), D_MODEL ** -0.5),
        'lambda_params': nrm(ks[16], (N_ODD, 4, C_QK_DIM), 0.1),
        'c_head_gain': 1.0 + nrm(ks[17], (N_ODD, C_V_DIM), 0.02),
        'w_out_odd': nrm(ks[18], (N_ODD, C_WIDTH, D_MODEL), C_WIDTH ** -0.5),
    }


def reference(x_prompt, x_sample, cache_a_k, cache_a_v, cache_a_logf, cache_b_k, cache_b_v, cache_b_kidx,
              cache_c_k, cache_c_v, norm_gain, final_gain, w_in_even, b_forget, w_out_even, w_in_odd,
              lambda_params, c_head_gain, w_out_odd):
    past_len = cache_a_k.shape[2]
    xp, xs = x_prompt, x_sample
    even_p, even_s, odd_p, odd_s = [], [], [], []
    for layer in range(DEPTH):
        g = norm_gain[layer]
        i = layer // 2
        if layer % 2 == 0:
            dp, st_p = even_mixer(rms_norm(xp, g), w_in_even[i], w_out_even[i], b_forget[i], 0, None)
            ds, st_s = even_mixer(rms_norm(xs, g), w_in_even[i], w_out_even[i], b_forget[i], past_len,
                                  (cache_a_k[i], cache_a_v[i], cache_a_logf[i],
                                   cache_b_k[i], cache_b_v[i], cache_b_kidx[i]))
            even_p.append(st_p)
            even_s.append(st_s)
        else:
            lam_init = 0.8 - 0.6 * math.exp(-0.3 * layer)
            dp, st_p = odd_mixer(rms_norm(xp, g), w_in_odd[i], lambda_params[i], c_head_gain[i], w_out_odd[i],
                                 lam_init, 0, None)
            ds, st_s = odd_mixer(rms_norm(xs, g), w_in_odd[i], lambda_params[i], c_head_gain[i], w_out_odd[i],
                                 lam_init, past_len, (cache_c_k[i], cache_c_v[i]))
            odd_p.append(st_p)
            odd_s.append(st_s)
        xp = xp + dp
        xs = xs + ds
    y_prompt = rms_norm(xp, final_gain)
    y_sample = rms_norm(xs, final_gain)
    a_k_p, a_v_p, a_f_p, b_k_p, b_v_p, b_i_p = [jnp.stack(t) for t in zip(*even_p)]
    a_k_s, a_v_s, a_f_s, b_k_s, b_v_s, b_i_s = [jnp.stack(t) for t in zip(*even_s)]
    c_k_p, c_v_p = [jnp.stack(t) for t in zip(*odd_p)]
    c_k_s, c_v_s = [jnp.stack(t) for t in zip(*odd_s)]
    return (y_prompt, y_sample,
            a_k_p, a_v_p, a_f_p, b_k_p, b_v_p, b_i_p, c_k_p, c_v_p,
            a_k_s, a_v_s, a_f_s, b_k_s, b_v_s, b_i_s, c_k_s, c_v_s)
```

```python
import functools
import math

import jax
import jax.numpy as jnp
import numpy as np
from jax import lax
from jax.experimental import pallas as pl
from jax.experimental.pallas import tpu as pltpu

LANES = 128
HEAD_DIM = 64
CHUNK = 64
CHUNK_SHIFT = 6
ROT_DIM = HEAD_DIM // 4
ROPE_THETA = 500000.0
IDX_TOPK_MAX = 256
N_HEADS = 8
NORM_EPS = 1e-6
NEG_INF = -1e30
QK_SCALE = HEAD_DIM ** -0.5
INT_MIN = -2 ** 31
VMEM_LIMIT = 56 * 1024 * 1024

f32 = jnp.float32
bf16 = jnp.bfloat16
i32 = jnp.int32


def _cparams(sem):
    return pltpu.CompilerParams(dimension_semantics=sem, vmem_limit_bytes=VMEM_LIMIT)


def _rope_group(r, cos, sa, sb):
    return r * cos + pltpu.roll(r, 8, 1) * sa + pltpu.roll(r, LANES - 8, 1) * sb


def _proj_kernel(plan, x_ref, g_ref, w_ref, cos_ref, sa_ref, sb_ref, bias_ref, *out_refs):
    x = x_ref[...]
    ms = jnp.mean(x * x, axis=-1, keepdims=True)
    h = (x * lax.rsqrt(ms + NORM_EPS) * g_ref[...]).astype(bf16)
    cos, sa, sb = cos_ref[...], sa_ref[...], sb_ref[...]
    lane = lax.broadcasted_iota(i32, (1, LANES), 1)
    oi = 0
    for (c0, width, kind, outs) in plan:
        r = jnp.dot(h, w_ref[:, c0:c0 + width], preferred_element_type=f32)
        if kind == "rope":
            r = jnp.concatenate(
                [_rope_group(r[:, g * LANES:(g + 1) * LANES], cos, sa, sb) for g in range(width // LANES)],
                axis=1)
        elif kind == "misc":
            low = lane < HEAD_DIM
            roped = _rope_group(r, jnp.where(low, cos, 1.0), jnp.where(low, sa, 0.0), jnp.where(low, sb, 0.0))
            z = r + bias_ref[...]
            logf = jnp.minimum(z, 0.0) - jnp.log(1.0 + jnp.exp(-jnp.abs(z)))
            r = jnp.where((lane >= HEAD_DIM) & (lane < HEAD_DIM + N_HEADS), logf, roped)
        for (dtype, scale) in outs:
            val = r if scale == 1.0 else r * scale
            out_refs[oi][...] = val.astype(dtype)
            oi += 1


def _proj(x2d, gain, w16, cos, sa, sb, bias, plan, tm):
    n, d = x2d.shape
    wtot = w16.shape[1]
    out_shapes, out_specs = [], []
    for (c0, width, kind, outs) in plan:
        for (dtype, _) in outs:
            out_shapes.append(jax.ShapeDtypeStruct((n, width), dtype))
            out_specs.append(pl.BlockSpec((tm, width), lambda i: (i, 0)))
    return pl.pallas_call(
        functools.partial(_proj_kernel, plan),
        grid=(n // tm,),
        in_specs=[
            pl.BlockSpec((tm, d), lambda i: (i, 0)),
            pl.BlockSpec((1, d), lambda i: (0, 0)),
            pl.BlockSpec((d, wtot), lambda i: (0, 0)),
            pl.BlockSpec((tm, LANES), lambda i: (i, 0)),
            pl.BlockSpec((tm, LANES), lambda i: (i, 0)),
            pl.BlockSpec((tm, LANES), lambda i: (i, 0)),
            pl.BlockSpec((1, LANES), lambda i: (0, 0)),
        ],
        out_specs=out_specs,
        out_shape=out_shapes,
        compiler_params=_cparams(("parallel",)),
        name="norm_in_proj",
    )(x2d, gain, w16, cos, sa, sb, bias)


EVEN_W = 4 * 512 + 4 * 512 + 512 + LANES
EVEN_PLAN = (
    (0, 512, "plain", ((bf16, QK_SCALE),)),
    (512, 512, "plain", ((f32, 1.0), (bf16, 1.0))),
    (1024, 512, "plain", ((f32, 1.0), (bf16, 1.0))),
    (1536, 512, "plain", ((f32, 1.0),)),
    (2048, 512, "rope", ((bf16, QK_SCALE),)),
    (2560, 512, "rope", ((f32, 1.0), (bf16, 1.0))),
    (3072, 512, "plain", ((f32, 1.0), (bf16, 1.0))),
    (3584, 512, "plain", ((f32, 1.0),)),
    (4096, 512, "rope", ((bf16, QK_SCALE),)),
    (4608, LANES, "misc", ((f32, 1.0),)),
)
ODD_PLAN = (
    (0, 1024, "rope", ((bf16, QK_SCALE),)),
    (1024, 1024, "rope", ((f32, 1.0), (bf16, 1.0))),
    (2048, 1024, "plain", ((f32, 1.0), (bf16, 1.0))),
    (3072, 1024, "plain", ((f32, 1.0),)),
)


def _reorder_even_weight(w):
    a4 = w[:, 0:2048]
    af = w[:, 2048:2056]
    b4 = w[:, 2056:4104]
    qi = w[:, 4104:4616]
    ki = w[:, 4616:4680]
    wi = w[:, 4680:4688]
    pad = jnp.zeros((w.shape[0], LANES - HEAD_DIM - 2 * N_HEADS), w.dtype)
    return jnp.concatenate([a4, b4, qi, ki, af, wi, pad], axis=1).astype(bf16)


def _rope_tables(pos):
    half = ROT_DIM // 2
    inv = ROPE_THETA ** (-jnp.arange(half, dtype=f32) * 2.0 / ROT_DIM)
    ang = pos.astype(f32)[:, None] * inv[None, :]
    cos, sin = jnp.cos(ang), jnp.sin(ang)
    n = pos.shape[0]
    rest = HEAD_DIM - ROT_DIM
    cos64 = jnp.concatenate([cos, cos, jnp.ones((n, rest), f32)], axis=1)
    sa64 = jnp.concatenate([jnp.zeros((n, half), f32), sin, jnp.zeros((n, rest), f32)], axis=1)
    sb64 = jnp.concatenate([-sin, jnp.zeros((n, half + rest), f32)], axis=1)
    return tuple(jnp.concatenate([t, t], axis=1) for t in (cos64, sa64, sb64))


def _cumsum_kernel(x_ref, o_ref):
    rows, length = x_ref.shape
    lane = lax.broadcasted_iota(i32, (rows, LANES), 1)

    def body(j, carry):
        off = pl.multiple_of(j * LANES, LANES)
        x = x_ref[:, pl.ds(off, LANES)]
        for s in (1, 2, 4, 8, 16, 32, 64):
            x = x + jnp.where(lane >= s, pltpu.roll(x, s, 1), 0.0)
        x = x + carry
        o_ref[:, pl.ds(off, LANES)] = x
        return jnp.broadcast_to(x[:, LANES - 1:LANES], (rows, LANES))

    lax.fori_loop(0, length // LANES, body, jnp.zeros((rows, LANES), f32))


def _cumsum_lanes(x):
    rows, length = x.shape
    return pl.pallas_call(
        _cumsum_kernel,
        grid=(rows // 8,),
        in_specs=[pl.BlockSpec((8, length), lambda i: (i, 0))],
        out_specs=pl.BlockSpec((8, length), lambda i: (i, 0)),
        out_shape=jax.ShapeDtypeStruct((rows, length), f32),
        compiler_params=_cparams(("parallel",)),
        name="logf_cumsum",
    )(x)


def _dot_nt(a, b):
    return lax.dot_general(a, b, (((1,), (1,)), ((), ())), preferred_element_type=f32)


def _lane_halves(q):
    lane = lax.broadcasted_iota(i32, q.shape, 1)
    zero = jnp.zeros_like(q)
    return jnp.where(lane < HEAD_DIM, q, zero), jnp.where(lane >= HEAD_DIM, q, zero)


def _softmax_step(s, v_blk, hidx, m_sc, l_sc, acc_sc):
    m_old = m_sc[hidx]
    m_new = jnp.maximum(m_old, jnp.max(s, axis=1, keepdims=True))
    alpha = jnp.exp(m_old - m_new)
    p = jnp.exp(s - m_new)
    l_sc[hidx] = alpha * l_sc[hidx] + jnp.sum(p, axis=1, keepdims=True)
    acc_sc[hidx] = alpha * acc_sc[hidx] + jnp.dot(p.astype(bf16), v_blk, preferred_element_type=f32)
    m_sc[hidx] = m_new


def _init_softmax(m_sc, l_sc, acc_sc):
    m_sc[...] = jnp.full(m_sc.shape, NEG_INF, f32)
    l_sc[...] = jnp.zeros(l_sc.shape, f32)
    acc_sc[...] = jnp.zeros(acc_sc.shape, f32)


def _chunk_block_range(first_q, tq, tk, lk):
    last_q = first_q + tq - 1
    n_full = (((first_q >> CHUNK_SHIFT) + 1) * CHUNK) // tk
    n_full = jnp.minimum(n_full, lk // tk)
    n_kb = ((((last_q >> CHUNK_SHIFT) + 1) * CHUNK - 1) // tk) + 1
    n_kb = jnp.minimum(n_kb, (lk + tk - 1) // tk)
    return n_full, n_kb


def _chunk_valid(first_q, k0, tq, tk, lk):
    q_pos = first_q + lax.broadcasted_iota(i32, (tq, 1), 0)
    k_pos = k0 + lax.broadcasted_iota(i32, (1, tk), 1)
    return ((k_pos >> CHUNK_SHIFT) <= (q_pos >> CHUNK_SHIFT)) & (k_pos < lk)


def _fox_kernel(q_ref, k_ref, v_ref, cq_ref, ck_ref, o_ref, m_sc, l_sc, acc_sc, *, tq, tk, past_len):
    qb = pl.program_id(2)
    first_q = past_len + qb * tq
    qh = _lane_halves(q_ref[0])
    cq = cq_ref[0, 0]
    _init_softmax(m_sc, l_sc, acc_sc)
    n_full = (first_q + 1) // tk
    n_kb = (first_q + tq - 1) // tk + 1

    def block(kb, masked):
        k0 = pl.multiple_of(kb * tk, tk)
        k_blk = k_ref[0, pl.ds(k0, tk), :]
        v_blk = v_ref[0, pl.ds(k0, tk), :]
        if masked:
            q_pos = first_q + lax.broadcasted_iota(i32, (tq, 1), 0)
            k_pos = k0 + lax.broadcasted_iota(i32, (1, tk), 1)
            visible = k_pos <= q_pos
        for hh in range(2):
            s = _dot_nt(qh[hh], k_blk)
            s = s + cq[:, hh:hh + 1] - ck_ref[0, 0, hh:hh + 1, pl.ds(k0, tk)]
            if masked:
                s = jnp.where(visible, s, NEG_INF)
            _softmax_step(s, v_blk, hh, m_sc, l_sc, acc_sc)

    def full_body(kb, c):
        block(kb, False)
        return c

    def masked_body(kb, c):
        block(kb, True)
        return c

    lax.fori_loop(0, n_full, full_body, 0)
    lax.fori_loop(n_full, n_kb, masked_body, 0)
    lane = lax.broadcasted_iota(i32, (tq, LANES), 1)
    o_ref[0] = jnp.where(lane < HEAD_DIM, acc_sc[0] / l_sc[0], acc_sc[1] / l_sc[1])


def _fox_attention(q16, k16, v16, cq4, ck4, past_len, tq, tk):
    b, lq, width = q16.shape
    lkp = k16.shape[1]
    ng = width // LANES
    return pl.pallas_call(
        functools.partial(_fox_kernel, tq=tq, tk=tk, past_len=past_len),
        grid=(b, ng, lq // tq),
        in_specs=[
            pl.BlockSpec((1, tq, LANES), lambda bi, g, qi: (bi, qi, g)),
            pl.BlockSpec((1, lkp, LANES), lambda bi, g, qi: (bi, 0, g)),
            pl.BlockSpec((1, lkp, LANES), lambda bi, g, qi: (bi, 0, g)),
            pl.BlockSpec((1, 1, tq, 2), lambda bi, g, qi: (bi, g, qi, 0)),
            pl.BlockSpec((1, 1, 2, lkp), lambda bi, g, qi: (bi, g, 0, 0)),
        ],
        out_specs=pl.BlockSpec((1, tq, LANES), lambda bi, g, qi: (bi, qi, g)),
        out_shape=jax.ShapeDtypeStruct((b, lq, width), f32),
        scratch_shapes=[
            pltpu.VMEM((2, tq, 1), f32),
            pltpu.VMEM((2, tq, 1), f32),
            pltpu.VMEM((2, tq, LANES), f32),
        ],
        compiler_params=_cparams(("parallel", "parallel", "arbitrary")),
        name="fox_attention",
    )(q16, k16, v16, cq4, ck4)


def _diff_kernel(q_ref, k_ref, v_ref, lp_ref, hg_ref, o_ref, m_sc, l_sc, acc_sc, *,
                 tq, tk, past_len, lk, lam_init):
    qb = pl.program_id(2)
    first_q = past_len + qb * tq
    qh = _lane_halves(q_ref[0])
    _init_softmax(m_sc, l_sc, acc_sc)
    n_full, n_kb = _chunk_block_range(first_q, tq, tk, lk)

    def block(kb, masked):
        k0 = pl.multiple_of(kb * tk, tk)
        k_blk = k_ref[0, pl.ds(k0, tk), :]
        v_blk = v_ref[0, pl.ds(k0, tk), :]
        if masked:
            valid = _chunk_valid(first_q, k0, tq, tk, lk)
        for c in range(2):
            s = _dot_nt(qh[c], k_blk)
            if masked:
                s = jnp.where(valid, s, NEG_INF)
            _softmax_step(s, v_blk, c, m_sc, l_sc, acc_sc)

    def full_body(kb, c):
        block(kb, False)
        return c

    def masked_body(kb, c):
        block(kb, True)
        return c

    lax.fori_loop(0, n_full, full_body, 0)
    lax.fori_loop(n_full, n_kb, masked_body, 0)

    lp = lp_ref[...]
    lam = (jnp.exp(jnp.sum(lp[0:1] * lp[1:2], axis=1, keepdims=True))
           - jnp.exp(jnp.sum(lp[2:3] * lp[3:4], axis=1, keepdims=True)) + lam_init)
    o = acc_sc[0] / l_sc[0] - lam * (acc_sc[1] / l_sc[1])
    ms = jnp.mean(o * o, axis=-1, keepdims=True)
    o_ref[0] = (o * lax.rsqrt(ms + NORM_EPS) * hg_ref[...]) * (1.0 - lam_init)


def _diff_attention(q16, k16, v16, lam_params, head_gain, lam_init, past_len, lk, tq, tk):
    b, lq, width = q16.shape
    lkp = k16.shape[1]
    nh = width // LANES
    return pl.pallas_call(
        functools.partial(_diff_kernel, tq=tq, tk=tk, past_len=past_len, lk=lk, lam_init=lam_init),
        grid=(b, nh, lq // tq),
        in_specs=[
            pl.BlockSpec((1, tq, LANES), lambda bi, h, qi: (bi, qi, h)),
            pl.BlockSpec((1, lkp, LANES), lambda bi, h, qi: (bi, 0, h)),
            pl.BlockSpec((1, lkp, LANES), lambda bi, h, qi: (bi, 0, h)),
            pl.BlockSpec((4, HEAD_DIM), lambda bi, h, qi: (0, 0)),
            pl.BlockSpec((1, LANES), lambda bi, h, qi: (0, 0)),
        ],
        out_specs=pl.BlockSpec((1, tq, LANES), lambda bi, h, qi: (bi, qi, h)),
        out_shape=jax.ShapeDtypeStruct((b, lq, width), f32),
        scratch_shapes=[
            pltpu.VMEM((2, tq, 1), f32),
            pltpu.VMEM((2, tq, 1), f32),
            pltpu.VMEM((2, tq, LANES), f32),
        ],
        compiler_params=_cparams(("parallel", "parallel", "arbitrary")),
        name="diff_attention",
    )(q16, k16, v16, lam_params, head_gain)


def _order_key(x):
    b = lax.bitcast_convert_type(x, i32)
    return b ^ ((b >> 31) & jnp.int32(0x7FFFFFFF))


def _dsa_kernel(q_ref, k_ref, v_ref, qi_ref, ki_ref, wi_ref, o_ref, key_sc, m_sc, l_sc, acc_sc, *,
                tq, tk, past_len, lk, topk, idx_bits):
    qb = pl.program_id(1)
    first_q = past_len + qb * tq
    n_full, n_kb = _chunk_block_range(first_q, tq, tk, lk)
    n_groups = tk // LANES

    def run_blocks(fn, init):
        c = lax.fori_loop(0, n_full, lambda kb, c: fn(kb, c, False), init)
        return lax.fori_loop(n_full, n_kb, lambda kb, c: fn(kb, c, True), c)

    qi_h = []
    for g in range(N_HEADS // 2):
        qi_h.extend(_lane_halves(qi_ref[0, :, g * LANES:(g + 1) * LANES]))
    wi = wi_ref[0] * (N_HEADS ** -0.5)

    def score_block(kb, c, masked):
        k0 = pl.multiple_of(kb * tk, tk)
        ki_blk = ki_ref[0, pl.ds(k0, tk), :]
        score = jnp.zeros((tq, tk), f32)
        for h in range(N_HEADS):
            d = _dot_nt(qi_h[h], ki_blk)
            score = score + jnp.maximum(d, 0.0) * wi[:, h:h + 1]
        if masked:
            score = jnp.where(_chunk_valid(first_q, k0, tq, tk, lk), score, NEG_INF)
        key_sc[:, pl.ds(k0, tk)] = _order_key(score)
        return c

    run_blocks(score_block, 0)

    def lane_fold(x):
        acc = x[:, 0:LANES]
        for g in range(1, n_groups):
            acc = acc + x[:, g * LANES:(g + 1) * LANES]
        return acc

    def count(indicator_fn):
        def body(kb, c):
            k0 = pl.multiple_of(kb * tk, tk)
            return c + lane_fold(indicator_fn(key_sc[:, pl.ds(k0, tk)], k0))
        c = lax.fori_loop(0, n_kb, body, jnp.zeros((tq, LANES), f32))
        return jnp.sum(c, axis=1, keepdims=True)

    def bit_body(i, thr):
        cand = thr ^ jnp.left_shift(jnp.int32(1), 31 - i)
        cnt = count(lambda key, k0: jnp.where(key >= cand, 1.0, 0.0))
        return jnp.where(cnt >= topk, cand, thr)

    thr = lax.fori_loop(0, 32, bit_body, jnp.full((tq, 1), INT_MIN, i32))

    cnt_gt = count(lambda key, k0: jnp.where(key > thr, 1.0, 0.0))
    cnt_eq = count(lambda key, k0: jnp.where(key == thr, 1.0, 0.0))
    need = topk - cnt_gt
    excess = jnp.max(jnp.where(cnt_eq > need, 1.0, 0.0)) > 0.0

    def tie_search():
        def tbit(i, last):
            cand = last | jnp.left_shift(jnp.int32(1), idx_bits - 1 - i)

            def indicator(key, k0):
                k_pos = k0 + lax.broadcasted_iota(i32, (1, tk), 1)
                return jnp.where(key == thr, jnp.where(k_pos < cand, 1.0, 0.0), 0.0)
            cnt = count(indicator)
            return jnp.where(cnt < need, cand, last)
        return lax.fori_loop(0, idx_bits, tbit, jnp.zeros((tq, 1), i32))

    tie_last = lax.cond(excess, tie_search, lambda: jnp.full((tq, 1), 2 ** idx_bits, i32))

    q_h = []
    for g in range(N_HEADS // 2):
        q_h.extend(_lane_halves(q_ref[0, :, g * LANES:(g + 1) * LANES]))
    _init_softmax(m_sc, l_sc, acc_sc)

    def attn_block(kb, c, masked):
        k0 = pl.multiple_of(kb * tk, tk)
        key = key_sc[:, pl.ds(k0, tk)]
        k_pos = k0 + lax.broadcasted_iota(i32, (1, tk), 1)
        tie_bias = jnp.where(k_pos <= tie_last, 0.0, NEG_INF)
        bias = jnp.where(key > thr, 0.0, jnp.where(key == thr, tie_bias, NEG_INF))
        if masked:
            bias = jnp.where(_chunk_valid(first_q, k0, tq, tk, lk), bias, NEG_INF)
        for h in range(N_HEADS):
            g = h // 2
            k_blk = k_ref[0, pl.ds(k0, tk), g * LANES:(g + 1) * LANES]
            v_blk = v_ref[0, pl.ds(k0, tk), g * LANES:(g + 1) * LANES]
            s = _dot_nt(q_h[h], k_blk) + bias
            _softmax_step(s, v_blk, h, m_sc, l_sc, acc_sc)
        return c

    run_blocks(attn_block, 0)
    lane = lax.broadcasted_iota(i32, (tq, LANES), 1)
    for g in range(N_HEADS // 2):
        o_ref[0, :, g * LANES:(g + 1) * LANES] = jnp.where(
            lane < HEAD_DIM, acc_sc[2 * g] / l_sc[2 * g], acc_sc[2 * g + 1] / l_sc[2 * g + 1])


def _dsa_attention(q16, k16, v16, qi16, ki2_16, wi, past_len, lk, tq, tk):
    b, lq, width = q16.shape
    lkp = k16.shape[1]
    topk = min(IDX_TOPK_MAX, lk // 4)
    idx_bits = max(1, int(lkp).bit_length())
    assert tk >= topk and lkp % tk == 0
    return pl.pallas_call(
        functools.partial(_dsa_kernel, tq=tq, tk=tk, past_len=past_len, lk=lk, topk=topk, idx_bits=idx_bits),
        grid=(b, lq // tq),
        in_specs=[
            pl.BlockSpec((1, tq, width), lambda bi, qi: (bi, qi, 0)),
            pl.BlockSpec((1, lkp, width), lambda bi, qi: (bi, 0, 0)),
            pl.BlockSpec((1, lkp, width), lambda bi, qi: (bi, 0, 0)),
            pl.BlockSpec((1, tq, width), lambda bi, qi: (bi, qi, 0)),
            pl.BlockSpec((1, lkp, LANES), lambda bi, qi: (bi, 0, 0)),
            pl.BlockSpec((1, tq, N_HEADS), lambda bi, qi: (bi, qi, 0)),
        ],
        out_specs=pl.BlockSpec((1, tq, width), lambda bi, qi: (bi, qi, 0)),
        out_shape=jax.ShapeDtypeStruct((b, lq, width), f32),
        scratch_shapes=[
            pltpu.VMEM((tq, lkp), i32),
            pltpu.VMEM((N_HEADS, tq, 1), f32),
            pltpu.VMEM((N_HEADS, tq, 1), f32),
            pltpu.VMEM((N_HEADS, tq, LANES), f32),
        ],
        compiler_params=_cparams(("parallel", "arbitrary")),
        name="dsa_attention",
    )(q16, k16, v16, qi16, ki2_16, wi)


def _out_kernel(n_parts, final, *refs):
    parts = refs[:2 * n_parts]
    w_ref, x_ref = refs[2 * n_parts], refs[2 * n_parts + 1]
    rest = refs[2 * n_parts + 2:]
    acc = x_ref[...]
    row = 0
    for i in range(n_parts):
        o, g = parts[2 * i][...], parts[2 * i + 1][...]
        mixed = (o * (g / (1.0 + jnp.exp(-g)))).astype(bf16)
        width = mixed.shape[1]
        acc = acc + jnp.dot(mixed, w_ref[row:row + width, :], preferred_element_type=f32)
        row += width
    if final:
        fg_ref, xo_ref, y_ref = rest
        ms = jnp.mean(acc * acc, axis=-1, keepdims=True)
        y_ref[...] = acc * lax.rsqrt(ms + NORM_EPS) * fg_ref[...]
    else:
        (xo_ref,) = rest
    xo_ref[...] = acc


def _out_proj(parts, w16, x2d, final_gain, tm):
    n, d = x2d.shape
    n_parts = len(parts)
    final = final_gain is not None
    in_specs, args = [], []
    for (o, g) in parts:
        for a in (o, g):
            in_specs.append(pl.BlockSpec((tm, a.shape[1]), lambda i: (i, 0)))
            args.append(a)
    in_specs += [pl.BlockSpec(w16.shape, lambda i: (0, 0)), pl.BlockSpec((tm, d), lambda i: (i, 0))]
    args += [w16, x2d]
    out_shape = [jax.ShapeDtypeStruct((n, d), f32)]
    out_specs = [pl.BlockSpec((tm, d), lambda i: (i, 0))]
    if final:
        in_specs.append(pl.BlockSpec((1, d), lambda i: (0, 0)))
        args.append(final_gain)
        out_shape.append(jax.ShapeDtypeStruct((n, d), f32))
        out_specs.append(pl.BlockSpec((tm, d), lambda i: (i, 0)))
    res = pl.pallas_call(
        functools.partial(_out_kernel, n_parts, final),
        grid=(n // tm,),
        in_specs=in_specs,
        out_specs=out_specs,
        out_shape=out_shape,
        compiler_params=_cparams(("parallel",)),
        name="gated_out_proj",
    )(*args)
    return res if final else (res[0], None)


def _row_tile(n):
    return 256 if n % 256 == 0 else n


def _attn_tiles(lq, lk):
    if lq >= 256 and lk % 512 == 0:
        return 256, 512, lk
    lkp = -(-lk // LANES) * LANES
    return lq, lkp, lkp


def _with_cache16(cache, new16, lkp):
    b, lq, width = new16.shape
    if cache is None:
        return new16
    past = cache.shape[1]
    pad = jnp.zeros((b, lkp - past - lq, width), bf16)
    return jnp.concatenate([cache.reshape(b, past, width).astype(bf16), new16, pad], axis=1)


def _even_layer(x, gain, w_in16, w_out16, b_forget, past_len, cache, final_gain):
    b, lq, d = x.shape
    n = b * lq
    lk = past_len + lq
    tq, tk, lkp = _attn_tiles(lq, lk)
    tm = _row_tile(n)
    pos = past_len + jnp.arange(lq)
    cos, sa, sb = (jnp.tile(t, (b, 1)) for t in _rope_tables(pos))
    bias = jnp.zeros((1, LANES), f32).at[0, HEAD_DIM:HEAD_DIM + N_HEADS].set(b_forget.astype(f32))
    (aq16, ak, ak16, av, av16, ag, bq16, bk, bk16, bv, bv16, bg, qi16, misc) = _proj(
        x.reshape(n, d), gain.reshape(1, d), w_in16, cos, sa, sb, bias, EVEN_PLAN, tm)
    ki = misc[:, :HEAD_DIM]
    logf = misc[:, HEAD_DIM:HEAD_DIM + N_HEADS]
    wi = misc[:, HEAD_DIM + N_HEADS:HEAD_DIM + 2 * N_HEADS]

    def r3(a):
        return a.reshape(b, lq, a.shape[-1])

    c_ak, c_av, c_af, c_bk, c_bv, c_bki = cache if cache is not None else (None,) * 6
    ki16 = r3(ki).astype(bf16)
    logf3 = r3(logf)
    if cache is not None:
        logf_all = jnp.concatenate([c_af.astype(f32), logf3], axis=1)
        logf_all = jnp.pad(logf_all, ((0, 0), (0, lkp - lk), (0, 0)))
    else:
        logf_all = logf3
    cum = _cumsum_lanes(jnp.swapaxes(logf_all, 1, 2).reshape(b * N_HEADS, lkp)).reshape(b, N_HEADS, lkp)
    ck4 = cum.reshape(b, N_HEADS // 2, 2, lkp)
    cq4 = jnp.swapaxes(cum[:, :, past_len:past_len + lq].reshape(b, N_HEADS // 2, 2, lq), 2, 3)

    a_out = _fox_attention(r3(aq16), _with_cache16(c_ak, r3(ak16), lkp), _with_cache16(c_av, r3(av16), lkp),
                           cq4, ck4, past_len, tq, tk)
    ki_all = _with_cache16(c_bki, ki16, lkp)
    ki2 = jnp.concatenate([ki_all, ki_all], axis=-1)
    b_out = _dsa_attention(r3(bq16), _with_cache16(c_bk, r3(bk16), lkp), _with_cache16(c_bv, r3(bv16), lkp),
                           r3(qi16), ki2, r3(wi), past_len, lk, tq, tk)
    x_new, y = _out_proj([(a_out.reshape(n, -1), ag), (b_out.reshape(n, -1), bg)], w_out16, x.reshape(n, d),
                         final_gain, tm)
    state = (ak.reshape(b, lq, N_HEADS, HEAD_DIM), av.reshape(b, lq, N_HEADS, HEAD_DIM), logf3,
             bk.reshape(b, lq, N_HEADS, HEAD_DIM), bv.reshape(b, lq, N_HEADS, HEAD_DIM), r3(ki))
    return x_new.reshape(b, lq, d), (None if y is None else y.reshape(b, lq, d)), state


def _odd_layer(x, gain, w_in16, lam_params, head_gain, w_out16, lam_init, past_len, cache, final_gain):
    b, lq, d = x.shape
    n = b * lq
    lk = past_len + lq
    tq, tk, lkp = _attn_tiles(lq, lk)
    tm = _row_tile(n)
    pos = past_len + jnp.arange(lq)
    cos, sa, sb = (jnp.tile(t, (b, 1)) for t in _rope_tables(pos))
    bias = jnp.zeros((1, LANES), f32)
    q16, k, k16, v, v16, g = _proj(x.reshape(n, d), gain.reshape(1, d), w_in16, cos, sa, sb, bias, ODD_PLAN, tm)

    def r3(a):
        return a.reshape(b, lq, a.shape[-1])

    c_ck, c_cv = cache if cache is not None else (None, None)
    o = _diff_attention(r3(q16), _with_cache16(c_ck, r3(k16), lkp), _with_cache16(c_cv, r3(v16), lkp),
                        lam_params.astype(f32), head_gain.reshape(1, LANES).astype(f32), lam_init,
                        past_len, lk, tq, tk)
    x_new, y = _out_proj([(o.reshape(n, -1), g)], w_out16, x.reshape(n, d), final_gain, tm)
    state = (k.reshape(b, lq, N_HEADS, 2 * HEAD_DIM), v.reshape(b, lq, N_HEADS, 2 * HEAD_DIM))
    return x_new.reshape(b, lq, d), (None if y is None else y.reshape(b, lq, d)), state


def kernel(x_prompt, x_sample, cache_a_k, cache_a_v, cache_a_logf, cache_b_k, cache_b_v, cache_b_kidx, cache_c_k, cache_c_v, norm_gain, final_gain, w_in_even, b_forget, w_out_even, w_in_odd, lambda_params, c_head_gain, w_out_odd):
    past_len = cache_a_k.shape[2]
    depth = norm_gain.shape[0]
    xp, xs = x_prompt, x_sample
    yp = ys = None
    even_p, even_s, odd_p, odd_s = [], [], [], []
    fg = final_gain.reshape(1, -1).astype(f32)
    for layer in range(depth):
        g = norm_gain[layer]
        i = layer // 2
        last = fg if layer == depth - 1 else None
        if layer % 2 == 0:
            w_in16 = _reorder_even_weight(w_in_even[i])
            w_out16 = w_out_even[i].astype(bf16)
            xp, yp, st_p = _even_layer(xp, g, w_in16, w_out16, b_forget[i], 0, None, last)
            xs, ys, st_s = _even_layer(xs, g, w_in16, w_out16, b_forget[i], past_len,
                                       (cache_a_k[i], cache_a_v[i], cache_a_logf[i],
                                        cache_b_k[i], cache_b_v[i], cache_b_kidx[i]), last)
            even_p.append(st_p)
            even_s.append(st_s)
        else:
            lam_init = 0.8 - 0.6 * math.exp(-0.3 * layer)
            w_in16 = w_in_odd[i].astype(bf16)
            w_out16 = w_out_odd[i].astype(bf16)
            xp, yp, st_p = _odd_layer(xp, g, w_in16, lambda_params[i], c_head_gain[i], w_out16, lam_init,
                                      0, None, last)
            xs, ys, st_s = _odd_layer(xs, g, w_in16, lambda_params[i], c_head_gain[i], w_out16, lam_init,
                                      past_len, (cache_c_k[i], cache_c_v[i]), last)
            odd_p.append(st_p)
            odd_s.append(st_s)
    a_k_p, a_v_p, a_f_p, b_k_p, b_v_p, b_i_p = [jnp.stack(t) for t in zip(*even_p)]
    a_k_s, a_v_s, a_f_s, b_k_s, b_v_s, b_i_s = [jnp.stack(t) for t in zip(*even_s)]
    c_k_p, c_v_p = [jnp.stack(t) for t in zip(*odd_p)]
    c_k_s, c_v_s = [jnp.stack(t) for t in zip(*odd_s)]
    return (yp, ys,
            a_k_p, a_v_p, a_f_p, b_k_p, b_v_p, b_i_p, c_k_p, c_v_p,
            a_k_s, a_v_s, a_f_s, b_k_s, b_v_s, b_i_s, c_k_s, c_v_s)
```

```python
import functools
import math

import jax
import jax.numpy as jnp
from jax import lax
from jax.experimental import pallas as pl
from jax.experimental.pallas import tpu as pltpu

LANES = 128
HEAD_DIM = 64
CHUNK = 64
CHUNK_SHIFT = 6
ROT_DIM = HEAD_DIM // 4
ROPE_THETA = 500000.0
IDX_TOPK_MAX = 256
N_HEADS = 8
NORM_EPS = 1e-6
NEG_INF = -1e30
QK_SCALE = HEAD_DIM ** -0.5
INT_MIN = -2 ** 31
VMEM_LIMIT = 56 * 1024 * 1024
REDUCE_CHAINS = 8
FOX_BIAS_COLS = 6

f32 = jnp.float32
bf16 = jnp.bfloat16
i32 = jnp.int32


def _cparams(sem):
    return pltpu.CompilerParams(dimension_semantics=sem, vmem_limit_bytes=VMEM_LIMIT)


def _rope_group(r, cos, sa, sb):
    return r * cos + pltpu.roll(r, 8, 1) * sa + pltpu.roll(r, LANES - 8, 1) * sb


def _proj_kernel(plan, x_ref, g_ref, w_ref, cos_ref, sa_ref, sb_ref, bias_ref, *out_refs):
    x = x_ref[...]
    ms = jnp.mean(x * x, axis=-1, keepdims=True)
    h = (x * lax.rsqrt(ms + NORM_EPS) * g_ref[...]).astype(bf16)
    cos, sa, sb = cos_ref[...], sa_ref[...], sb_ref[...]
    lane = lax.broadcasted_iota(i32, (1, LANES), 1)
    oi = 0
    for (c0, width, kind, outs) in plan:
        r = jnp.dot(h, w_ref[:, c0:c0 + width], preferred_element_type=f32)
        if kind == "rope":
            r = jnp.concatenate(
                [_rope_group(r[:, g * LANES:(g + 1) * LANES], cos, sa, sb) for g in range(width // LANES)],
                axis=1)
        elif kind == "misc":
            low = lane < HEAD_DIM
            roped = _rope_group(r, jnp.where(low, cos, 1.0), jnp.where(low, sa, 0.0), jnp.where(low, sb, 0.0))
            z = r + bias_ref[...]
            logf = jnp.minimum(z, 0.0) - jnp.log(1.0 + jnp.exp(-jnp.abs(z)))
            r = jnp.where((lane >= HEAD_DIM) & (lane < HEAD_DIM + N_HEADS), logf, roped)
        for (dtype, scale) in outs:
            val = r if scale == 1.0 else r * scale
            out_refs[oi][...] = val.astype(dtype)
            oi += 1


def _proj(x2d, gain, w16, cos, sa, sb, bias, plan, tm):
    n, d = x2d.shape
    wtot = w16.shape[1]
    out_shapes, out_specs = [], []
    for (c0, width, kind, outs) in plan:
        for (dtype, _) in outs:
            out_shapes.append(jax.ShapeDtypeStruct((n, width), dtype))
            out_specs.append(pl.BlockSpec((tm, width), lambda i: (i, 0)))
    return pl.pallas_call(
        functools.partial(_proj_kernel, plan),
        grid=(n // tm,),
        in_specs=[
            pl.BlockSpec((tm, d), lambda i: (i, 0)),
            pl.BlockSpec((1, d), lambda i: (0, 0)),
            pl.BlockSpec((d, wtot), lambda i: (0, 0)),
            pl.BlockSpec((tm, LANES), lambda i: (i, 0)),
            pl.BlockSpec((tm, LANES), lambda i: (i, 0)),
            pl.BlockSpec((tm, LANES), lambda i: (i, 0)),
            pl.BlockSpec((1, LANES), lambda i: (0, 0)),
        ],
        out_specs=out_specs,
        out_shape=out_shapes,
        compiler_params=_cparams(("parallel",)),
        name="norm_in_proj",
    )(x2d, gain, w16, cos, sa, sb, bias)


EVEN_PLAN = (
    (0, 512, "plain", ((bf16, QK_SCALE),)),
    (512, 512, "plain", ((f32, 1.0), (bf16, 1.0))),
    (1024, 512, "plain", ((f32, 1.0), (bf16, 1.0))),
    (1536, 512, "plain", ((f32, 1.0),)),
    (2048, 512, "rope", ((bf16, QK_SCALE),)),
    (2560, 512, "rope", ((f32, 1.0), (bf16, 1.0))),
    (3072, 512, "plain", ((f32, 1.0), (bf16, 1.0))),
    (3584, 512, "plain", ((f32, 1.0),)),
    (4096, 512, "rope", ((bf16, QK_SCALE),)),
    (4608, LANES, "misc", ((f32, 1.0),)),
)
ODD_PLAN = (
    (0, 1024, "rope", ((bf16, QK_SCALE),)),
    (1024, 1024, "rope", ((f32, 1.0), (bf16, 1.0))),
    (2048, 1024, "plain", ((f32, 1.0), (bf16, 1.0))),
    (3072, 1024, "plain", ((f32, 1.0),)),
)


def _reorder_even_weight(w):
    a4 = w[:, 0:2048]
    af = w[:, 2048:2056]
    b4 = w[:, 2056:4104]
    qi = w[:, 4104:4616]
    ki = w[:, 4616:4680]
    wi = w[:, 4680:4688]
    pad = jnp.zeros((w.shape[0], LANES - HEAD_DIM - 2 * N_HEADS), w.dtype)
    return jnp.concatenate([a4, b4, qi, ki, af, wi, pad], axis=1).astype(bf16)


def _rope_tables(pos):
    half = ROT_DIM // 2
    inv = ROPE_THETA ** (-jnp.arange(half, dtype=f32) * 2.0 / ROT_DIM)
    ang = pos.astype(f32)[:, None] * inv[None, :]
    cos, sin = jnp.cos(ang), jnp.sin(ang)
    n = pos.shape[0]
    rest = HEAD_DIM - ROT_DIM
    cos64 = jnp.concatenate([cos, cos, jnp.ones((n, rest), f32)], axis=1)
    sa64 = jnp.concatenate([jnp.zeros((n, half), f32), sin, jnp.zeros((n, rest), f32)], axis=1)
    sb64 = jnp.concatenate([-sin, jnp.zeros((n, half + rest), f32)], axis=1)
    return tuple(jnp.concatenate([t, t], axis=1) for t in (cos64, sa64, sb64))


def _cumsum_kernel(x_ref, o_ref):
    rows, length = x_ref.shape
    lane = lax.broadcasted_iota(i32, (rows, LANES), 1)

    def body(j, carry):
        off = pl.multiple_of(j * LANES, LANES)
        x = x_ref[:, pl.ds(off, LANES)]
        for s in (1, 2, 4, 8, 16, 32, 64):
            x = x + jnp.where(lane >= s, pltpu.roll(x, s, 1), 0.0)
        x = x + carry
        o_ref[:, pl.ds(off, LANES)] = x
        return jnp.broadcast_to(x[:, LANES - 1:LANES], (rows, LANES))

    lax.fori_loop(0, length // LANES, body, jnp.zeros((rows, LANES), f32))


def _cumsum_lanes(x):
    rows, length = x.shape
    return pl.pallas_call(
        _cumsum_kernel,
        grid=(rows // 8,),
        in_specs=[pl.BlockSpec((8, length), lambda i: (i, 0))],
        out_specs=pl.BlockSpec((8, length), lambda i: (i, 0)),
        out_shape=jax.ShapeDtypeStruct((rows, length), f32),
        compiler_params=_cparams(("parallel",)),
        name="logf_cumsum",
    )(x)


def _dot_nt(a, b):
    return lax.dot_general(a, b, (((1,), (1,)), ((), ())), preferred_element_type=f32)


def _lane_halves(q):
    lane = lax.broadcasted_iota(i32, q.shape, 1)
    zero = jnp.zeros_like(q)
    return jnp.where(lane < HEAD_DIM, q, zero), jnp.where(lane >= HEAD_DIM, q, zero)


def _softmax_step(s, v_blk, hidx, m_sc, l_sc, acc_sc):
    m_old = m_sc[hidx]
    m_new = jnp.maximum(m_old, jnp.max(s, axis=1, keepdims=True))
    alpha = jnp.exp(m_old - m_new)
    p = jnp.exp(s - m_new)
    l_sc[hidx] = alpha * l_sc[hidx] + jnp.sum(p, axis=1, keepdims=True)
    acc_sc[hidx] = alpha * acc_sc[hidx] + jnp.dot(p.astype(bf16), v_blk, preferred_element_type=f32)
    m_sc[hidx] = m_new


def _init_softmax(m_sc, l_sc, acc_sc):
    m_sc[...] = jnp.full(m_sc.shape, NEG_INF, f32)
    l_sc[...] = jnp.zeros(l_sc.shape, f32)
    acc_sc[...] = jnp.zeros(acc_sc.shape, f32)


def _chunk_block_range(first_q, tq, tk, lk):
    last_q = first_q + tq - 1
    n_full = (((first_q >> CHUNK_SHIFT) + 1) * CHUNK) // tk
    n_full = jnp.minimum(n_full, lk // tk)
    n_kb = ((((last_q >> CHUNK_SHIFT) + 1) * CHUNK - 1) // tk) + 1
    n_kb = jnp.minimum(n_kb, (lk + tk - 1) // tk)
    return n_full, n_kb


def _chunk_valid(first_q, k0, tq, tk, lk):
    q_pos = first_q + lax.broadcasted_iota(i32, (tq, 1), 0)
    k_pos = k0 + lax.broadcasted_iota(i32, (1, tk), 1)
    return ((k_pos >> CHUNK_SHIFT) <= (q_pos >> CHUNK_SHIFT)) & (k_pos < lk)


def _order_key(x):
    b = lax.bitcast_convert_type(x, i32)
    return b ^ ((b >> 31) & jnp.int32(0x7FFFFFFF))


def _fox_kernel(q_ref, k_ref, v_ref, cq_ref, ck_ref, o_ref, m_sc, l_sc, acc_sc, *, tq, tk, past_len):
    qb = pl.program_id(2)
    first_q = past_len + qb * tq
    qh = _lane_halves(q_ref[0])
    cq = cq_ref[0, 0]
    _init_softmax(m_sc, l_sc, acc_sc)
    n_full = (first_q + 1) // tk
    n_kb = (first_q + tq - 1) // tk + 1

    def block(kb, masked):
        k0 = pl.multiple_of(kb * tk, tk)
        k_blk = k_ref[0, pl.ds(k0, tk), :]
        v_blk = v_ref[0, pl.ds(k0, tk), :]
        if masked:
            q_pos = first_q + lax.broadcasted_iota(i32, (tq, 1), 0)
            k_pos = k0 + lax.broadcasted_iota(i32, (1, tk), 1)
            visible = k_pos <= q_pos
        for hh in range(2):
            s = _dot_nt(qh[hh], k_blk)
            s = s + cq[:, hh:hh + 1] - ck_ref[0, 0, hh:hh + 1, pl.ds(k0, tk)]
            if masked:
                s = jnp.where(visible, s, NEG_INF)
            _softmax_step(s, v_blk, hh, m_sc, l_sc, acc_sc)

    def full_body(kb, c):
        block(kb, False)
        return c

    def masked_body(kb, c):
        block(kb, True)
        return c

    lax.fori_loop(0, n_full, full_body, 0)
    lax.fori_loop(n_full, n_kb, masked_body, 0)
    lane = lax.broadcasted_iota(i32, (tq, LANES), 1)
    o_ref[0] = jnp.where(lane < HEAD_DIM, acc_sc[0] / l_sc[0], acc_sc[1] / l_sc[1])


def _fox_attention(q16, k16, v16, cq4, ck4, past_len, tq, tk):
    b, lq, width = q16.shape
    lkp = k16.shape[1]
    ng = width // LANES
    return pl.pallas_call(
        functools.partial(_fox_kernel, tq=tq, tk=tk, past_len=past_len),
        grid=(b, ng, lq // tq),
        in_specs=[
            pl.BlockSpec((1, tq, LANES), lambda bi, g, qi: (bi, qi, g)),
            pl.BlockSpec((1, lkp, LANES), lambda bi, g, qi: (bi, 0, g)),
            pl.BlockSpec((1, lkp, LANES), lambda bi, g, qi: (bi, 0, g)),
            pl.BlockSpec((1, 1, tq, 2), lambda bi, g, qi: (bi, g, qi, 0)),
            pl.BlockSpec((1, 1, 2, lkp), lambda bi, g, qi: (bi, g, 0, 0)),
        ],
        out_specs=pl.BlockSpec((1, tq, LANES), lambda bi, g, qi: (bi, qi, g)),
        out_shape=jax.ShapeDtypeStruct((b, lq, width), f32),
        scratch_shapes=[
            pltpu.VMEM((2, tq, 1), f32),
            pltpu.VMEM((2, tq, 1), f32),
            pltpu.VMEM((2, tq, LANES), f32),
        ],
        compiler_params=_cparams(("parallel", "parallel", "arbitrary")),
        name="fox_attention",
    )(q16, k16, v16, cq4, ck4)


def _diff_kernel(q_ref, k_ref, v_ref, lp_ref, hg_ref, o_ref, m_sc, l_sc, acc_sc, *,
                 tq, tk, past_len, lk, lam_init):
    qb = pl.program_id(2)
    first_q = past_len + qb * tq
    qh = _lane_halves(q_ref[0])
    _init_softmax(m_sc, l_sc, acc_sc)
    n_full, n_kb = _chunk_block_range(first_q, tq, tk, lk)

    def block(kb, masked):
        k0 = pl.multiple_of(kb * tk, tk)
        k_blk = k_ref[0, pl.ds(k0, tk), :]
        v_blk = v_ref[0, pl.ds(k0, tk), :]
        if masked:
            valid = _chunk_valid(first_q, k0, tq, tk, lk)
        for c in range(2):
            s = _dot_nt(qh[c], k_blk)
            if masked:
                s = jnp.where(valid, s, NEG_INF)
            _softmax_step(s, v_blk, c, m_sc, l_sc, acc_sc)

    def full_body(kb, c):
        block(kb, False)
        return c

    def masked_body(kb, c):
        block(kb, True)
        return c

    lax.fori_loop(0, n_full, full_body, 0)
    lax.fori_loop(n_full, n_kb, masked_body, 0)
    o = acc_sc[0] / l_sc[0] - _diff_lambda(lp_ref, lam_init) * (acc_sc[1] / l_sc[1])
    o_ref[0] = _head_norm(o, hg_ref, lam_init)


def _diff_lambda(lp_ref, lam_init):
    lp = lp_ref[...]
    return (jnp.exp(jnp.sum(lp[0:1] * lp[1:2], axis=1, keepdims=True))
            - jnp.exp(jnp.sum(lp[2:3] * lp[3:4], axis=1, keepdims=True)) + lam_init)


def _head_norm(o, hg_ref, lam_init):
    ms = jnp.mean(o * o, axis=-1, keepdims=True)
    return (o * lax.rsqrt(ms + NORM_EPS) * hg_ref[...]) * (1.0 - lam_init)


def _diff_attention(q16, k16, v16, lam_params, head_gain, lam_init, past_len, lk, tq, tk):
    b, lq, width = q16.shape
    lkp = k16.shape[1]
    nh = width // LANES
    return pl.pallas_call(
        functools.partial(_diff_kernel, tq=tq, tk=tk, past_len=past_len, lk=lk, lam_init=lam_init),
        grid=(b, nh, lq // tq),
        in_specs=[
            pl.BlockSpec((1, tq, LANES), lambda bi, h, qi: (bi, qi, h)),
            pl.BlockSpec((1, lkp, LANES), lambda bi, h, qi: (bi, 0, h)),
            pl.BlockSpec((1, lkp, LANES), lambda bi, h, qi: (bi, 0, h)),
            pl.BlockSpec((4, HEAD_DIM), lambda bi, h, qi: (0, 0)),
            pl.BlockSpec((1, LANES), lambda bi, h, qi: (0, 0)),
        ],
        out_specs=pl.BlockSpec((1, tq, LANES), lambda bi, h, qi: (bi, qi, h)),
        out_shape=jax.ShapeDtypeStruct((b, lq, width), f32),
        scratch_shapes=[
            pltpu.VMEM((2, tq, 1), f32),
            pltpu.VMEM((2, tq, 1), f32),
            pltpu.VMEM((2, tq, LANES), f32),
        ],
        compiler_params=_cparams(("parallel", "parallel", "arbitrary")),
        name="diff_attention",
    )(q16, k16, v16, lam_params, head_gain)


def _dsa_kernel(q_ref, k_ref, v_ref, qi_ref, ki_ref, wi_ref, o_ref, key_sc, m_sc, l_sc, acc_sc, *,
                tq, tk, past_len, lk, topk, idx_bits):
    qb = pl.program_id(1)
    first_q = past_len + qb * tq
    n_full, n_kb = _chunk_block_range(first_q, tq, tk, lk)
    n_groups = tk // LANES

    def run_blocks(fn, init):
        c = lax.fori_loop(0, n_full, lambda kb, c: fn(kb, c, False), init)
        return lax.fori_loop(n_full, n_kb, lambda kb, c: fn(kb, c, True), c)

    qi_h = []
    for g in range(N_HEADS // 2):
        qi_h.extend(_lane_halves(qi_ref[0, :, g * LANES:(g + 1) * LANES]))
    wi = wi_ref[0] * (N_HEADS ** -0.5)

    def score_block(kb, c, masked):
        k0 = pl.multiple_of(kb * tk, tk)
        ki_blk = ki_ref[0, pl.ds(k0, tk), :]
        score = jnp.zeros((tq, tk), f32)
        for h in range(N_HEADS):
            d = _dot_nt(qi_h[h], ki_blk)
            score = score + jnp.maximum(d, 0.0) * wi[:, h:h + 1]
        if masked:
            score = jnp.where(_chunk_valid(first_q, k0, tq, tk, lk), score, NEG_INF)
        key_sc[:, pl.ds(k0, tk)] = _order_key(score)
        return c

    run_blocks(score_block, 0)

    def lane_fold(x):
        acc = x[:, 0:LANES]
        for g in range(1, n_groups):
            acc = acc + x[:, g * LANES:(g + 1) * LANES]
        return acc

    def count(indicator_fn):
        def body(kb, c):
            k0 = pl.multiple_of(kb * tk, tk)
            return c + lane_fold(indicator_fn(key_sc[:, pl.ds(k0, tk)], k0))
        c = lax.fori_loop(0, n_kb, body, jnp.zeros((tq, LANES), f32))
        return jnp.sum(c, axis=1, keepdims=True)

    def bit_body(i, thr):
        cand = thr ^ jnp.left_shift(jnp.int32(1), 31 - i)
        cnt = count(lambda key, k0: jnp.where(key >= cand, 1.0, 0.0))
        return jnp.where(cnt >= topk, cand, thr)

    thr = lax.fori_loop(0, 32, bit_body, jnp.full((tq, 1), INT_MIN, i32))

    cnt_gt = count(lambda key, k0: jnp.where(key > thr, 1.0, 0.0))
    cnt_eq = count(lambda key, k0: jnp.where(key == thr, 1.0, 0.0))
    need = topk - cnt_gt
    excess = jnp.max(jnp.where(cnt_eq > need, 1.0, 0.0)) > 0.0

    def tie_search():
        def tbit(i, last):
            cand = last | jnp.left_shift(jnp.int32(1), idx_bits - 1 - i)

            def indicator(key, k0):
                k_pos = k0 + lax.broadcasted_iota(i32, (1, tk), 1)
                return jnp.where(key == thr, jnp.where(k_pos < cand, 1.0, 0.0), 0.0)
            cnt = count(indicator)
            return jnp.where(cnt < need, cand, last)
        return lax.fori_loop(0, idx_bits, tbit, jnp.zeros((tq, 1), i32))

    tie_last = lax.cond(excess, tie_search, lambda: jnp.full((tq, 1), 2 ** idx_bits, i32))

    q_h = []
    for g in range(N_HEADS // 2):
        q_h.extend(_lane_halves(q_ref[0, :, g * LANES:(g + 1) * LANES]))
    _init_softmax(m_sc, l_sc, acc_sc)

    def attn_block(kb, c, masked):
        k0 = pl.multiple_of(kb * tk, tk)
        key = key_sc[:, pl.ds(k0, tk)]
        k_pos = k0 + lax.broadcasted_iota(i32, (1, tk), 1)
        tie_bias = jnp.where(k_pos <= tie_last, 0.0, NEG_INF)
        bias = jnp.where(key > thr, 0.0, jnp.where(key == thr, tie_bias, NEG_INF))
        if masked:
            bias = jnp.where(_chunk_valid(first_q, k0, tq, tk, lk), bias, NEG_INF)
        for h in range(N_HEADS):
            g = h // 2
            k_blk = k_ref[0, pl.ds(k0, tk), g * LANES:(g + 1) * LANES]
            v_blk = v_ref[0, pl.ds(k0, tk), g * LANES:(g + 1) * LANES]
            s = _dot_nt(q_h[h], k_blk) + bias
            _softmax_step(s, v_blk, h, m_sc, l_sc, acc_sc)
        return c

    run_blocks(attn_block, 0)
    lane = lax.broadcasted_iota(i32, (tq, LANES), 1)
    for g in range(N_HEADS // 2):
        o_ref[0, :, g * LANES:(g + 1) * LANES] = jnp.where(
            lane < HEAD_DIM, acc_sc[2 * g] / l_sc[2 * g], acc_sc[2 * g + 1] / l_sc[2 * g + 1])


def _dsa_attention(q16, k16, v16, qi16, ki2_16, wi, past_len, lk, tq, tk):
    b, lq, width = q16.shape
    lkp = k16.shape[1]
    topk = min(IDX_TOPK_MAX, lk // 4)
    idx_bits = max(1, int(lkp).bit_length())
    assert tk >= topk and lkp % tk == 0
    return pl.pallas_call(
        functools.partial(_dsa_kernel, tq=tq, tk=tk, past_len=past_len, lk=lk, topk=topk, idx_bits=idx_bits),
        grid=(b, lq // tq),
        in_specs=[
            pl.BlockSpec((1, tq, width), lambda bi, qi: (bi, qi, 0)),
            pl.BlockSpec((1, lkp, width), lambda bi, qi: (bi, 0, 0)),
            pl.BlockSpec((1, lkp, width), lambda bi, qi: (bi, 0, 0)),
            pl.BlockSpec((1, tq, width), lambda bi, qi: (bi, qi, 0)),
            pl.BlockSpec((1, lkp, LANES), lambda bi, qi: (bi, 0, 0)),
            pl.BlockSpec((1, tq, N_HEADS), lambda bi, qi: (bi, qi, 0)),
        ],
        out_specs=pl.BlockSpec((1, tq, width), lambda bi, qi: (bi, qi, 0)),
        out_shape=jax.ShapeDtypeStruct((b, lq, width), f32),
        scratch_shapes=[
            pltpu.VMEM((tq, lkp), i32),
            pltpu.VMEM((N_HEADS, tq, 1), f32),
            pltpu.VMEM((N_HEADS, tq, 1), f32),
            pltpu.VMEM((N_HEADS, tq, LANES), f32),
        ],
        compiler_params=_cparams(("parallel", "arbitrary")),
        name="dsa_attention",
    )(q16, k16, v16, qi16, ki2_16, wi)


def _resident_spec(block_shape, index_map):
    return pl.BlockSpec(block_shape, index_map, pipeline_mode=pl.Buffered(1))


def _tree_fold(x, op):
    slab = x.shape[0] // REDUCE_CHAINS
    parts = [x[i * slab:(i + 1) * slab] for i in range(REDUCE_CHAINS)]
    while len(parts) > 1:
        parts = [op(parts[i], parts[i + 1]) for i in range(0, len(parts), 2)]
    return parts[0]


def _softmax_step_t(s, v_t, hidx, m_sc, l_sc, acc_sc):
    m_old = m_sc[hidx]
    m_new = jnp.maximum(m_old, jnp.max(_tree_fold(s, jnp.maximum), axis=0, keepdims=True))
    alpha = jnp.exp(m_old - m_new)
    p = jnp.exp(s - m_new)
    l_sc[hidx] = alpha * l_sc[hidx] + jnp.sum(_tree_fold(p, jnp.add), axis=0, keepdims=True)
    acc_sc[hidx] = alpha * acc_sc[hidx] + jnp.dot(v_t, p.astype(bf16), preferred_element_type=f32)
    m_sc[hidx] = m_new


def _score_tile(buf, k_ref, lane0, k0, tk, q_t):
    half = tk // 2
    for i in range(2):
        buf[i * half:(i + 1) * half, :] = jnp.dot(
            k_ref[0, pl.ds(k0 + i * half, half), lane0:lane0 + LANES], q_t, preferred_element_type=f32)


def _score_pipeline(n_chains, n_kb, qk_fn, sm_fn, bufs):
    assert n_chains % 2 == 0
    qk_fn(0, 0, bufs[0])

    def body(kb, carry):
        for c in range(n_chains):
            if c + 1 < n_chains:
                qk_fn(kb, c + 1, bufs[(c + 1) % 2])
            else:
                qk_fn(kb + 1, 0, bufs[0])
            sm_fn(kb, c, bufs[c % 2], False)
        return carry

    lax.fori_loop(0, n_kb - 1, body, 0)
    for c in range(n_chains):
        if c + 1 < n_chains:
            qk_fn(n_kb - 1, c + 1, bufs[(c + 1) % 2])
        sm_fn(n_kb - 1, c, bufs[c % 2], True)


def _half_rows(x, upper):
    row = lax.broadcasted_iota(i32, x.shape, 0)
    keep = (row >= HEAD_DIM) if upper else (row < HEAD_DIM)
    return jnp.where(keep, x, jnp.zeros_like(x))


def _chunk_valid_t(first_q, k0, tq, tk, lk):
    k_pos = k0 + lax.broadcasted_iota(i32, (tk, tq), 0)
    q_pos = first_q + lax.broadcasted_iota(i32, (tk, tq), 1)
    return ((k_pos >> CHUNK_SHIFT) <= (q_pos >> CHUNK_SHIFT)) & (k_pos < lk)


def _store_head_pairs_t(o_ref, heads_t):
    for g in range(len(heads_t) // 2):
        o_t = jnp.concatenate([heads_t[2 * g], heads_t[2 * g + 1]], axis=0)
        o_ref[0, :, g * LANES:(g + 1) * LANES] = o_t.T


def _fox_t_kernel(q_ref, k_ref, v_ref, o_ref, m_sc, l_sc, acc_sc, sa_sc, sb_sc, *, tq, tk, past_len, nh):
    qb = pl.program_id(2)
    first_q = past_len + qb * tq
    n_kb = first_q // tk + 1
    _init_softmax(m_sc, l_sc, acc_sc)

    def qk(kb, c, buf):
        k0 = pl.multiple_of(kb * tk, tk)
        _score_tile(buf, k_ref, c * LANES, k0, tk, q_ref[0, c * LANES:(c + 1) * LANES, :])

    def sm(kb, c, buf, masked):
        k0 = pl.multiple_of(kb * tk, tk)
        s = buf[...]
        if masked:
            k_pos = k0 + lax.broadcasted_iota(i32, (tk, tq), 0)
            q_pos = first_q + lax.broadcasted_iota(i32, (tk, tq), 1)
            s = jnp.where(k_pos <= q_pos, s, NEG_INF)
        _softmax_step_t(s, v_ref[0, c * HEAD_DIM:(c + 1) * HEAD_DIM, pl.ds(k0, tk)], c, m_sc, l_sc, acc_sc)

    _score_pipeline(nh, n_kb, qk, sm, (sa_sc, sb_sc))
    _store_head_pairs_t(o_ref, [acc_sc[c] / l_sc[c] for c in range(nh)])


def _fox_attention_t(q_aug_t, k_aug, v_t, past_len, tq, tk, nh):
    b, _, lq = q_aug_t.shape
    lkp = k_aug.shape[1]
    assert tk % tq == 0 and past_len % tk == 0 and lkp % tk == 0
    return pl.pallas_call(
        functools.partial(_fox_t_kernel, tq=tq, tk=tk, past_len=past_len, nh=nh),
        grid=(b, N_HEADS // nh, lq // tq),
        in_specs=[
            pl.BlockSpec((1, nh * LANES, tq), lambda bi, g, qi: (bi, g, qi)),
            _resident_spec((1, lkp, nh * LANES), lambda bi, g, qi: (bi, 0, g)),
            _resident_spec((1, nh * HEAD_DIM, lkp), lambda bi, g, qi: (bi, g, 0)),
        ],
        out_specs=pl.BlockSpec((1, tq, nh * HEAD_DIM), lambda bi, g, qi: (bi, qi, g)),
        out_shape=jax.ShapeDtypeStruct((b, lq, N_HEADS * HEAD_DIM), f32),
        scratch_shapes=[
            pltpu.VMEM((nh, 1, tq), f32),
            pltpu.VMEM((nh, 1, tq), f32),
            pltpu.VMEM((nh, HEAD_DIM, tq), f32),
            pltpu.VMEM((tk, tq), f32),
            pltpu.VMEM((tk, tq), f32),
        ],
        compiler_params=_cparams(("parallel", "parallel", "arbitrary")),
        name="fox_attention_t",
    )(q_aug_t, k_aug, v_t)


def _bf16_prefix(x):
    bits = lax.bitcast_convert_type(x, jnp.uint32) & jnp.uint32(0xFFFF0000)
    return lax.bitcast_convert_type(bits, f32)


def _split3(x):
    x1 = _bf16_prefix(x)
    r = x - x1
    x2 = _bf16_prefix(r)
    return x1.astype(bf16), x2.astype(bf16), (r - x2).astype(bf16)


def _fox_operands_t(aq16, ak16, av16, cum, past_len):
    b, lq, _ = aq16.shape
    lk = ak16.shape[1]
    pad = LANES - HEAD_DIM - FOX_BIAS_COLS
    ck = [jnp.swapaxes(c, 1, 2)[..., None] for c in _split3(cum)]
    one_k = jnp.ones((b, lk, N_HEADS, 1), bf16)
    k_aug = jnp.concatenate(
        [ak16.reshape(b, lk, N_HEADS, HEAD_DIM), one_k, one_k, one_k, -ck[0], -ck[1], -ck[2],
         jnp.zeros((b, lk, N_HEADS, pad), bf16)], axis=-1).reshape(b, lk, N_HEADS * LANES)
    cq = [c[:, :, None, :] for c in _split3(cum[:, :, past_len:past_len + lq])]
    one_q = jnp.ones((b, N_HEADS, 1, lq), bf16)
    q_t = jnp.transpose(aq16.reshape(b, lq, N_HEADS, HEAD_DIM), (0, 2, 3, 1))
    q_aug_t = jnp.concatenate(
        [q_t, cq[0], cq[1], cq[2], one_q, one_q, one_q, jnp.zeros((b, N_HEADS, pad, lq), bf16)],
        axis=2).reshape(b, N_HEADS * LANES, lq)
    return q_aug_t, k_aug, jnp.swapaxes(av16, 1, 2)


def _diff_t_kernel(q_ref, k_ref, v_ref, lp_ref, hg_ref, o_ref, m_sc, l_sc, acc_sc, sa_sc, sb_sc, *,
                   tq, tk, past_len, lk, lam_init, nh):
    qb = pl.program_id(2)
    first_q = past_len + qb * tq
    n_kb = first_q // tk + 1
    _init_softmax(m_sc, l_sc, acc_sc)

    def qk(kb, c, buf):
        k0 = pl.multiple_of(kb * tk, tk)
        h = c // 2
        q_c = _half_rows(q_ref[0, h * LANES:(h + 1) * LANES, :], c % 2 == 1)
        _score_tile(buf, k_ref, h * LANES, k0, tk, q_c)

    def sm(kb, c, buf, masked):
        k0 = pl.multiple_of(kb * tk, tk)
        h = c // 2
        s = buf[...]
        if masked:
            s = jnp.where(_chunk_valid_t(first_q, k0, tq, tk, lk), s, NEG_INF)
        _softmax_step_t(s, v_ref[0, h * LANES:(h + 1) * LANES, pl.ds(k0, tk)], c, m_sc, l_sc, acc_sc)

    _score_pipeline(2 * nh, n_kb, qk, sm, (sa_sc, sb_sc))
    lam = _diff_lambda(lp_ref, lam_init)
    for h in range(nh):
        o_t = acc_sc[2 * h] / l_sc[2 * h] - lam * (acc_sc[2 * h + 1] / l_sc[2 * h + 1])
        o_ref[0, :, h * LANES:(h + 1) * LANES] = _head_norm(o_t.T, hg_ref, lam_init)


def _diff_attention_t(q_t, k16, v_t, lam_params, head_gain, lam_init, past_len, lk, tq, tk, nh):
    b, width, lq = q_t.shape
    lkp = k16.shape[1]
    n_heads = width // LANES
    assert tk % tq == 0 and past_len % tk == 0 and lkp % tk == 0 and tq % CHUNK == 0
    return pl.pallas_call(
        functools.partial(_diff_t_kernel, tq=tq, tk=tk, past_len=past_len, lk=lk, lam_init=lam_init, nh=nh),
        grid=(b, n_heads // nh, lq // tq),
        in_specs=[
            pl.BlockSpec((1, nh * LANES, tq), lambda bi, g, qi: (bi, g, qi)),
            _resident_spec((1, lkp, nh * LANES), lambda bi, g, qi: (bi, 0, g)),
            _resident_spec((1, nh * LANES, lkp), lambda bi, g, qi: (bi, g, 0)),
            pl.BlockSpec((4, HEAD_DIM), lambda bi, g, qi: (0, 0)),
            pl.BlockSpec((1, LANES), lambda bi, g, qi: (0, 0)),
        ],
        out_specs=pl.BlockSpec((1, tq, nh * LANES), lambda bi, g, qi: (bi, qi, g)),
        out_shape=jax.ShapeDtypeStruct((b, lq, width), f32),
        scratch_shapes=[
            pltpu.VMEM((2 * nh, 1, tq), f32),
            pltpu.VMEM((2 * nh, 1, tq), f32),
            pltpu.VMEM((2 * nh, LANES, tq), f32),
            pltpu.VMEM((tk, tq), f32),
            pltpu.VMEM((tk, tq), f32),
        ],
        compiler_params=_cparams(("parallel", "parallel", "arbitrary")),
        name="diff_attention_t",
    )(q_t, k16, v_t, lam_params, head_gain)


def _fold_rows(x, rows_out, op):
    n = x.shape[0] // rows_out
    chains = min(FOLD_CHAINS, n)
    accs = [x[j * rows_out:(j + 1) * rows_out] for j in range(chains)]
    for i in range(chains, n):
        accs[i % chains] = op(accs[i % chains], x[i * rows_out:(i + 1) * rows_out])
    while len(accs) > 1:
        accs = [op(accs[i], accs[i + 1]) for i in range(0, len(accs), 2)]
    return accs[0]


FOLD_CHAINS = 4
HALF_BIAS = 32768
PACKED_ROWS = 16


def _dsa_t_kernel(q_ref, k_ref, v_ref, qi_ref, ki_ref, wi_ref, o_ref, hi_sc, lo_sc, bias_sc, m_sc, l_sc, acc_sc,
                  sa_sc, sb_sc, *, tq, tk, past_len, lk, topk, idx_bits):
    qb = pl.program_id(1)
    first_q = past_len + qb * tq
    n_kb = first_q // tk + 1

    def head_rows(ref, h):
        g = h // 2
        return _half_rows(ref[0, g * LANES:(g + 1) * LANES, :], h % 2 == 1)

    wi = wi_ref[0] * (N_HEADS ** -0.5)

    def score_block(kb, masked):
        k0 = pl.multiple_of(kb * tk, tk)
        ki_blk = ki_ref[0, pl.ds(k0, tk), :]
        score = jnp.zeros((tk, tq), f32)
        for h in range(N_HEADS):
            d = jnp.dot(ki_blk, head_rows(qi_ref, h), preferred_element_type=f32)
            score = score + jnp.maximum(d, 0.0) * wi[h:h + 1, :]
        if masked:
            score = jnp.where(_chunk_valid_t(first_q, k0, tq, tk, lk), score, NEG_INF)
        key = _order_key(score)
        hi_sc[pl.ds(k0, tk), :] = (key >> 16).astype(jnp.int16)
        lo_sc[pl.ds(k0, tk), :] = ((key & 0xFFFF) - HALF_BIAS).astype(jnp.int16)

    def score_body(kb, c):
        score_block(kb, False)
        return c

    lax.fori_loop(0, n_kb - 1, score_body, 0)
    score_block(n_kb - 1, True)

    one16, zero16 = jnp.int16(1), jnp.int16(0)

    def count16(indicator_fn):
        def body(kb, c):
            k0 = pl.multiple_of(kb * tk, tk)
            ind = indicator_fn(hi_sc[pl.ds(k0, tk), :], lo_sc[pl.ds(k0, tk), :])
            return c + _fold_rows(ind, PACKED_ROWS, jnp.add)
        c = lax.fori_loop(0, n_kb, body, jnp.zeros((PACKED_ROWS, tq), jnp.int16))
        return jnp.sum(c.astype(i32), axis=0, keepdims=True)

    def search16(use_lo, need):
        def bit_body(i, t_off):
            cand = t_off | jnp.left_shift(jnp.int32(1), 15 - i)
            c16 = (cand - HALF_BIAS).astype(jnp.int16)
            cnt = count16(lambda hi, lo: jnp.where((lo if use_lo else hi) >= c16, one16, zero16))
            return jnp.where(cnt >= need, cand, t_off)
        return lax.fori_loop(0, 16, bit_body, jnp.zeros((1, tq), i32))

    thi = search16(False, topk) - HALF_BIAS
    thi16 = thi.astype(jnp.int16)
    cnt_hi_gt = count16(lambda hi, lo: jnp.where(hi > thi16, one16, zero16))

    def keep_lo_body(kb, c):
        k0 = pl.multiple_of(kb * tk, tk)
        lo_sc[pl.ds(k0, tk), :] = jnp.where(hi_sc[pl.ds(k0, tk), :] == thi16, lo_sc[pl.ds(k0, tk), :],
                                            jnp.int16(-HALF_BIAS))
        return c

    lax.fori_loop(0, n_kb, keep_lo_body, 0)
    tlo_off = search16(True, topk - cnt_hi_gt)
    tlo16 = (tlo_off - HALF_BIAS).astype(jnp.int16)
    thr = (thi << 16) | tlo_off

    cnt_gt = cnt_hi_gt + count16(lambda hi, lo: jnp.where(lo > tlo16, one16, zero16))
    cnt_eq = count16(lambda hi, lo: jnp.where(hi == thi16, jnp.where(lo == tlo16, one16, zero16), zero16))
    need = topk - cnt_gt
    excess = jnp.max(jnp.where(cnt_eq > need, 1, 0)) > 0

    def key_block(k0):
        return (hi_sc[pl.ds(k0, tk), :].astype(i32) << 16) | (lo_sc[pl.ds(k0, tk), :].astype(i32) + HALF_BIAS)

    def tie_search():
        def tbit(i, last):
            cand = last | jnp.left_shift(jnp.int32(1), idx_bits - 1 - i)

            def body(kb, c):
                k0 = pl.multiple_of(kb * tk, tk)
                k_pos = k0 + lax.broadcasted_iota(i32, (tk, tq), 0)
                ind = jnp.where(key_block(k0) == thr, jnp.where(k_pos < cand, 1, 0), 0)
                return c + jnp.sum(_tree_fold(ind, jnp.add), axis=0, keepdims=True)
            cnt = lax.fori_loop(0, n_kb, body, jnp.zeros((1, tq), i32))
            return jnp.where(cnt < need, cand, last)
        return lax.fori_loop(0, idx_bits, tbit, jnp.zeros((1, tq), i32))

    tie_last = lax.cond(excess, tie_search, lambda: jnp.full((1, tq), 2 ** idx_bits, i32))

    _init_softmax(m_sc, l_sc, acc_sc)

    def qk(kb, h, buf):
        k0 = pl.multiple_of(kb * tk, tk)
        _score_tile(buf, k_ref, (h // 2) * LANES, k0, tk, head_rows(q_ref, h))

    def sm(kb, h, buf, masked):
        k0 = pl.multiple_of(kb * tk, tk)
        if h == 0:
            def finish(bias):
                if masked:
                    bias = jnp.where(_chunk_valid_t(first_q, k0, tq, tk, lk), bias, NEG_INF)
                bias_sc[...] = bias

            @pl.when(jnp.logical_not(excess))
            def _():
                hi, lo = hi_sc[pl.ds(k0, tk), :], lo_sc[pl.ds(k0, tk), :]
                zero_b, neg_b = jnp.bfloat16(0.0), jnp.bfloat16(NEG_INF)
                low_sel = jnp.where(lo >= tlo16, zero_b, neg_b)
                finish(jnp.where(hi > thi16, zero_b, jnp.where(hi == thi16, low_sel, neg_b)).astype(f32))

            @pl.when(excess)
            def _():
                key = key_block(k0)
                k_pos = k0 + lax.broadcasted_iota(i32, (tk, tq), 0)
                tie_bias = jnp.where(k_pos <= tie_last, 0.0, NEG_INF)
                finish(jnp.where(key > thr, 0.0, jnp.where(key == thr, tie_bias, NEG_INF)))
        s = buf[...] + bias_sc[...]
        _softmax_step_t(s, v_ref[0, h * HEAD_DIM:(h + 1) * HEAD_DIM, pl.ds(k0, tk)], h, m_sc, l_sc, acc_sc)

    _score_pipeline(N_HEADS, n_kb, qk, sm, (sa_sc, sb_sc))
    _store_head_pairs_t(o_ref, [acc_sc[h] / l_sc[h] for h in range(N_HEADS)])


def _dsa_attention_t(q_t, k16, v_t, qi_t, ki2_16, wi_t, past_len, lk, tq, tk):
    b, width, lq = q_t.shape
    lkp = k16.shape[1]
    topk = min(IDX_TOPK_MAX, lk // 4)
    idx_bits = max(1, int(lkp).bit_length())
    assert tk >= topk and lkp % tk == 0 and tk % tq == 0 and past_len % tk == 0 and tq % CHUNK == 0
    return pl.pallas_call(
        functools.partial(_dsa_t_kernel, tq=tq, tk=tk, past_len=past_len, lk=lk, topk=topk, idx_bits=idx_bits),
        grid=(b, lq // tq),
        in_specs=[
            pl.BlockSpec((1, width, tq), lambda bi, qi: (bi, 0, qi)),
            _resident_spec((1, lkp, width), lambda bi, qi: (bi, 0, 0)),
            _resident_spec((1, width, lkp), lambda bi, qi: (bi, 0, 0)),
            pl.BlockSpec((1, width, tq), lambda bi, qi: (bi, 0, qi)),
            _resident_spec((1, lkp, LANES), lambda bi, qi: (bi, 0, 0)),
            pl.BlockSpec((1, N_HEADS, tq), lambda bi, qi: (bi, 0, qi)),
        ],
        out_specs=pl.BlockSpec((1, tq, width), lambda bi, qi: (bi, qi, 0)),
        out_shape=jax.ShapeDtypeStruct((b, lq, width), f32),
        scratch_shapes=[
            pltpu.VMEM((lkp, tq), jnp.int16),
            pltpu.VMEM((lkp, tq), jnp.int16),
            pltpu.VMEM((tk, tq), f32),
            pltpu.VMEM((N_HEADS, 1, tq), f32),
            pltpu.VMEM((N_HEADS, 1, tq), f32),
            pltpu.VMEM((N_HEADS, HEAD_DIM, tq), f32),
            pltpu.VMEM((tk, tq), f32),
            pltpu.VMEM((tk, tq), f32),
        ],
        compiler_params=_cparams(("parallel", "arbitrary")),
        name="dsa_attention_t",
    )(q_t, k16, v_t, qi_t, ki2_16, wi_t)


def _out_kernel(n_parts, final, *refs):
    parts = refs[:2 * n_parts]
    w_ref, x_ref = refs[2 * n_parts], refs[2 * n_parts + 1]
    rest = refs[2 * n_parts + 2:]
    acc = x_ref[...]
    row = 0
    for i in range(n_parts):
        o, g = parts[2 * i][...], parts[2 * i + 1][...]
        mixed = (o * (g / (1.0 + jnp.exp(-g)))).astype(bf16)
        width = mixed.shape[1]
        acc = acc + jnp.dot(mixed, w_ref[row:row + width, :], preferred_element_type=f32)
        row += width
    if final:
        fg_ref, xo_ref, y_ref = rest
        ms = jnp.mean(acc * acc, axis=-1, keepdims=True)
        y_ref[...] = acc * lax.rsqrt(ms + NORM_EPS) * fg_ref[...]
    else:
        (xo_ref,) = rest
    xo_ref[...] = acc


def _out_proj(parts, w16, x2d, final_gain, tm):
    n, d = x2d.shape
    n_parts = len(parts)
    final = final_gain is not None
    in_specs, args = [], []
    for (o, g) in parts:
        for a in (o, g):
            in_specs.append(pl.BlockSpec((tm, a.shape[1]), lambda i: (i, 0)))
            args.append(a)
    in_specs += [pl.BlockSpec(w16.shape, lambda i: (0, 0)), pl.BlockSpec((tm, d), lambda i: (i, 0))]
    args += [w16, x2d]
    out_shape = [jax.ShapeDtypeStruct((n, d), f32)]
    out_specs = [pl.BlockSpec((tm, d), lambda i: (i, 0))]
    if final:
        in_specs.append(pl.BlockSpec((1, d), lambda i: (0, 0)))
        args.append(final_gain)
        out_shape.append(jax.ShapeDtypeStruct((n, d), f32))
        out_specs.append(pl.BlockSpec((tm, d), lambda i: (i, 0)))
    res = pl.pallas_call(
        functools.partial(_out_kernel, n_parts, final),
        grid=(n // tm,),
        in_specs=in_specs,
        out_specs=out_specs,
        out_shape=out_shape,
        compiler_params=_cparams(("parallel",)),
        name="gated_out_proj",
    )(*args)
    return res if final else (res[0], None)


KEY_MAJOR_TILE = 512
FOX_HEADS_PER_STEP = 4
DIFF_HEADS_PER_STEP = 2


def _row_tile(n):
    return 256 if n % 256 == 0 else n


def _key_major(lq, lk, past_len):
    return lq % KEY_MAJOR_TILE == 0 and lk % KEY_MAJOR_TILE == 0 and past_len % KEY_MAJOR_TILE == 0


def _padded_keys(lq, lk, past_len):
    return lk if _key_major(lq, lk, past_len) else -(-lk // LANES) * LANES


def _with_cache16(cache, new16, lkp):
    b, lq, width = new16.shape
    if cache is None:
        return new16
    past = cache.shape[1]
    pad = jnp.zeros((b, lkp - past - lq, width), bf16)
    return jnp.concatenate([cache.reshape(b, past, width).astype(bf16), new16, pad], axis=1)


def _even_layer(x, gain, w_in16, w_out16, b_forget, past_len, cache, final_gain):
    b, lq, d = x.shape
    n = b * lq
    lk = past_len + lq
    lkp = _padded_keys(lq, lk, past_len)
    tm = _row_tile(n)
    pos = past_len + jnp.arange(lq)
    cos, sa, sb = (jnp.tile(t, (b, 1)) for t in _rope_tables(pos))
    bias = jnp.zeros((1, LANES), f32).at[0, HEAD_DIM:HEAD_DIM + N_HEADS].set(b_forget.astype(f32))
    (aq16, ak, ak16, av, av16, ag, bq16, bk, bk16, bv, bv16, bg, qi16, misc) = _proj(
        x.reshape(n, d), gain.reshape(1, d), w_in16, cos, sa, sb, bias, EVEN_PLAN, tm)
    ki = misc[:, :HEAD_DIM]
    logf = misc[:, HEAD_DIM:HEAD_DIM + N_HEADS]
    wi = misc[:, HEAD_DIM + N_HEADS:HEAD_DIM + 2 * N_HEADS]

    def r3(a):
        return a.reshape(b, lq, a.shape[-1])

    c_ak, c_av, c_af, c_bk, c_bv, c_bki = cache if cache is not None else (None,) * 6
    ki16 = r3(ki).astype(bf16)
    logf3 = r3(logf)
    if cache is not None:
        logf_all = jnp.concatenate([c_af.astype(f32), logf3], axis=1)
        logf_all = jnp.pad(logf_all, ((0, 0), (0, lkp - lk), (0, 0)))
    else:
        logf_all = logf3
    cum = _cumsum_lanes(jnp.swapaxes(logf_all, 1, 2).reshape(b * N_HEADS, lkp)).reshape(b, N_HEADS, lkp)

    ak_all, av_all = _with_cache16(c_ak, r3(ak16), lkp), _with_cache16(c_av, r3(av16), lkp)
    bk_all, bv_all = _with_cache16(c_bk, r3(bk16), lkp), _with_cache16(c_bv, r3(bv16), lkp)
    ki_all = _with_cache16(c_bki, ki16, lkp)
    ki2 = jnp.concatenate([ki_all, ki_all], axis=-1)
    if _key_major(lq, lk, past_len):
        q_aug_t, k_aug, v_t = _fox_operands_t(r3(aq16), ak_all, av_all, cum, past_len)
        a_out = _fox_attention_t(q_aug_t, k_aug, v_t, past_len, KEY_MAJOR_TILE, KEY_MAJOR_TILE, FOX_HEADS_PER_STEP)
        b_out = _dsa_attention_t(jnp.swapaxes(r3(bq16), 1, 2), bk_all, jnp.swapaxes(bv_all, 1, 2),
                                 jnp.swapaxes(r3(qi16), 1, 2), ki2, jnp.swapaxes(r3(wi), 1, 2),
                                 past_len, lk, KEY_MAJOR_TILE, KEY_MAJOR_TILE)
    else:
        ck4 = cum.reshape(b, N_HEADS // 2, 2, lkp)
        cq4 = jnp.swapaxes(cum[:, :, past_len:past_len + lq].reshape(b, N_HEADS // 2, 2, lq), 2, 3)
        a_out = _fox_attention(r3(aq16), ak_all, av_all, cq4, ck4, past_len, lq, lkp)
        b_out = _dsa_attention(r3(bq16), bk_all, bv_all, r3(qi16), ki2, r3(wi), past_len, lk, lq, lkp)
    x_new, y = _out_proj([(a_out.reshape(n, -1), ag), (b_out.reshape(n, -1), bg)], w_out16, x.reshape(n, d),
                         final_gain, tm)
    state = (ak.reshape(b, lq, N_HEADS, HEAD_DIM), av.reshape(b, lq, N_HEADS, HEAD_DIM), logf3,
             bk.reshape(b, lq, N_HEADS, HEAD_DIM), bv.reshape(b, lq, N_HEADS, HEAD_DIM), r3(ki))
    return x_new.reshape(b, lq, d), (None if y is None else y.reshape(b, lq, d)), state


def _odd_layer(x, gain, w_in16, lam_params, head_gain, w_out16, lam_init, past_len, cache, final_gain):
    b, lq, d = x.shape
    n = b * lq
    lk = past_len + lq
    lkp = _padded_keys(lq, lk, past_len)
    tm = _row_tile(n)
    pos = past_len + jnp.arange(lq)
    cos, sa, sb = (jnp.tile(t, (b, 1)) for t in _rope_tables(pos))
    bias = jnp.zeros((1, LANES), f32)
    q16, k, k16, v, v16, g = _proj(x.reshape(n, d), gain.reshape(1, d), w_in16, cos, sa, sb, bias, ODD_PLAN, tm)

    def r3(a):
        return a.reshape(b, lq, a.shape[-1])

    c_ck, c_cv = cache if cache is not None else (None, None)
    k_all, v_all = _with_cache16(c_ck, r3(k16), lkp), _with_cache16(c_cv, r3(v16), lkp)
    lam_args = (lam_params.astype(f32), head_gain.reshape(1, LANES).astype(f32), lam_init, past_len, lk)
    if _key_major(lq, lk, past_len):
        o = _diff_attention_t(jnp.swapaxes(r3(q16), 1, 2), k_all, jnp.swapaxes(v_all, 1, 2), *lam_args,
                              KEY_MAJOR_TILE, KEY_MAJOR_TILE, DIFF_HEADS_PER_STEP)
    else:
        o = _diff_attention(r3(q16), k_all, v_all, *lam_args, lq, lkp)
    x_new, y = _out_proj([(o.reshape(n, -1), g)], w_out16, x.reshape(n, d), final_gain, tm)
    state = (k.reshape(b, lq, N_HEADS, 2 * HEAD_DIM), v.reshape(b, lq, N_HEADS, 2 * HEAD_DIM))
    return x_new.reshape(b, lq, d), (None if y is None else y.reshape(b, lq, d)), state


def kernel(x_prompt, x_sample, cache_a_k, cache_a_v, cache_a_logf, cache_b_k, cache_b_v, cache_b_kidx, cache_c_k, cache_c_v, norm_gain, final_gain, w_in_even, b_forget, w_out_even, w_in_odd, lambda_params, c_head_gain, w_out_odd):
    past_len = cache_a_k.shape[2]
    depth = norm_gain.shape[0]
    xp, xs = x_prompt, x_sample
    yp = ys = None
    even_p, even_s, odd_p, odd_s = [], [], [], []
    fg = final_gain.reshape(1, -1).astype(f32)
    for layer in range(depth):
        g = norm_gain[layer]
        i = layer // 2
        last = fg if layer == depth - 1 else None
        if layer % 2 == 0:
            w_in16 = _reorder_even_weight(w_in_even[i])
            w_out16 = w_out_even[i].astype(bf16)
            xp, yp, st_p = _even_layer(xp, g, w_in16, w_out16, b_forget[i], 0, None, last)
            xs, ys, st_s = _even_layer(xs, g, w_in16, w_out16, b_forget[i], past_len,
                                       (cache_a_k[i], cache_a_v[i], cache_a_logf[i],
                                        cache_b_k[i], cache_b_v[i], cache_b_kidx[i]), last)
            even_p.append(st_p)
            even_s.append(st_s)
        else:
            lam_init = 0.8 - 0.6 * math.exp(-0.3 * layer)
            w_in16 = w_in_odd[i].astype(bf16)
            w_out16 = w_out_odd[i].astype(bf16)
            xp, yp, st_p = _odd_layer(xp, g, w_in16, lambda_params[i], c_head_gain[i], w_out16, lam_init,
                                      0, None, last)
            xs, ys, st_s = _odd_layer(xs, g, w_in16, lambda_params[i], c_head_gain[i], w_out16, lam_init,
                                      past_len, (cache_c_k[i], cache_c_v[i]), last)
            odd_p.append(st_p)
            odd_s.append(st_s)
    a_k_p, a_v_p, a_f_p, b_k_p, b_v_p, b_i_p = [jnp.stack(t) for t in zip(*even_p)]
    a_k_s, a_v_s, a_f_s, b_k_s, b_v_s, b_i_s = [jnp.stack(t) for t in zip(*even_s)]
    c_k_p, c_v_p = [jnp.stack(t) for t in zip(*odd_p)]
    c_k_s, c_v_s = [jnp.stack(t) for t in zip(*odd_s)]
    return (yp, ys,
            a_k_p, a_v_p, a_f_p, b_k_p, b_v_p, b_i_p, c_k_p, c_v_p,
            a_k_s, a_v_s, a_f_s, b_k_s, b_v_s, b_i_s, c_k_s, c_v_s)
```

```python
import functools
import math

import jax
import jax.numpy as jnp
from jax import lax
from jax.experimental import pallas as pl
from jax.experimental.pallas import tpu as pltpu

LANES = 128
HEAD_DIM = 64
CHUNK = 64
CHUNK_SHIFT = 6
ROT_DIM = HEAD_DIM // 4
ROPE_THETA = 500000.0
IDX_TOPK_MAX = 256
N_HEADS = 8
NORM_EPS = 1e-6
NEG_INF = -1e30
QK_SCALE = HEAD_DIM ** -0.5
INT_MIN = -2 ** 31
VMEM_LIMIT = 56 * 1024 * 1024
REDUCE_CHAINS = 8
FOX_BIAS_COLS = 6

f32 = jnp.float32
bf16 = jnp.bfloat16
i32 = jnp.int32


def _cparams(sem):
    return pltpu.CompilerParams(dimension_semantics=sem, vmem_limit_bytes=VMEM_LIMIT)


def _rope_group(r, cos, sa, sb):
    return r * cos + pltpu.roll(r, 8, 1) * sa + pltpu.roll(r, LANES - 8, 1) * sb


def _proj_kernel(plan, x_ref, g_ref, w_ref, cos_ref, sa_ref, sb_ref, bias_ref, *out_refs):
    x = x_ref[...]
    ms = jnp.mean(x * x, axis=-1, keepdims=True)
    h = (x * lax.rsqrt(ms + NORM_EPS) * g_ref[...]).astype(bf16)
    cos, sa, sb = cos_ref[...], sa_ref[...], sb_ref[...]
    lane = lax.broadcasted_iota(i32, (1, LANES), 1)
    oi = 0
    for (c0, width, kind, outs) in plan:
        r = jnp.dot(h, w_ref[:, c0:c0 + width], preferred_element_type=f32)
        if kind == "rope":
            r = jnp.concatenate(
                [_rope_group(r[:, g * LANES:(g + 1) * LANES], cos, sa, sb) for g in range(width // LANES)],
                axis=1)
        elif kind == "misc":
            low = lane < HEAD_DIM
            roped = _rope_group(r, jnp.where(low, cos, 1.0), jnp.where(low, sa, 0.0), jnp.where(low, sb, 0.0))
            z = r + bias_ref[...]
            logf = jnp.minimum(z, 0.0) - jnp.log(1.0 + jnp.exp(-jnp.abs(z)))
            r = jnp.where((lane >= HEAD_DIM) & (lane < HEAD_DIM + N_HEADS), logf, roped)
        for (dtype, scale) in outs:
            val = r if scale == 1.0 else r * scale
            out_refs[oi][...] = val.astype(dtype)
            oi += 1


def _proj(x2d, gain, w16, cos, sa, sb, bias, plan, tm):
    n, d = x2d.shape
    wtot = w16.shape[1]
    out_shapes, out_specs = [], []
    for (c0, width, kind, outs) in plan:
        for (dtype, _) in outs:
            out_shapes.append(jax.ShapeDtypeStruct((n, width), dtype))
            out_specs.append(pl.BlockSpec((tm, width), lambda i: (i, 0)))
    return pl.pallas_call(
        functools.partial(_proj_kernel, plan),
        grid=(n // tm,),
        in_specs=[
            pl.BlockSpec((tm, d), lambda i: (i, 0)),
            pl.BlockSpec((1, d), lambda i: (0, 0)),
            pl.BlockSpec((d, wtot), lambda i: (0, 0)),
            pl.BlockSpec((tm, LANES), lambda i: (i, 0)),
            pl.BlockSpec((tm, LANES), lambda i: (i, 0)),
            pl.BlockSpec((tm, LANES), lambda i: (i, 0)),
            pl.BlockSpec((1, LANES), lambda i: (0, 0)),
        ],
        out_specs=out_specs,
        out_shape=out_shapes,
        compiler_params=_cparams(("parallel",)),
        name="norm_in_proj",
    )(x2d, gain, w16, cos, sa, sb, bias)


EVEN_PLAN = (
    (0, 512, "plain", ((bf16, QK_SCALE),)),
    (512, 512, "plain", ((f32, 1.0), (bf16, 1.0))),
    (1024, 512, "plain", ((f32, 1.0), (bf16, 1.0))),
    (1536, 512, "plain", ((f32, 1.0),)),
    (2048, 512, "rope", ((bf16, QK_SCALE),)),
    (2560, 512, "rope", ((f32, 1.0), (bf16, 1.0))),
    (3072, 512, "plain", ((f32, 1.0), (bf16, 1.0))),
    (3584, 512, "plain", ((f32, 1.0),)),
    (4096, 512, "rope", ((bf16, QK_SCALE),)),
    (4608, LANES, "misc", ((f32, 1.0),)),
)
ODD_PLAN = (
    (0, 1024, "rope", ((bf16, QK_SCALE),)),
    (1024, 1024, "rope", ((f32, 1.0), (bf16, 1.0))),
    (2048, 1024, "plain", ((f32, 1.0), (bf16, 1.0))),
    (3072, 1024, "plain", ((f32, 1.0),)),
)


def _reorder_even_weight(w):
    a4 = w[:, 0:2048]
    af = w[:, 2048:2056]
    b4 = w[:, 2056:4104]
    qi = w[:, 4104:4616]
    ki = w[:, 4616:4680]
    wi = w[:, 4680:4688]
    pad = jnp.zeros((w.shape[0], LANES - HEAD_DIM - 2 * N_HEADS), w.dtype)
    return jnp.concatenate([a4, b4, qi, ki, af, wi, pad], axis=1).astype(bf16)


def _rope_tables(pos):
    half = ROT_DIM // 2
    inv = ROPE_THETA ** (-jnp.arange(half, dtype=f32) * 2.0 / ROT_DIM)
    ang = pos.astype(f32)[:, None] * inv[None, :]
    cos, sin = jnp.cos(ang), jnp.sin(ang)
    n = pos.shape[0]
    rest = HEAD_DIM - ROT_DIM
    cos64 = jnp.concatenate([cos, cos, jnp.ones((n, rest), f32)], axis=1)
    sa64 = jnp.concatenate([jnp.zeros((n, half), f32), sin, jnp.zeros((n, rest), f32)], axis=1)
    sb64 = jnp.concatenate([-sin, jnp.zeros((n, half + rest), f32)], axis=1)
    return tuple(jnp.concatenate([t, t], axis=1) for t in (cos64, sa64, sb64))


def _cumsum_kernel(x_ref, o_ref):
    rows, length = x_ref.shape
    lane = lax.broadcasted_iota(i32, (rows, LANES), 1)

    def body(j, carry):
        off = pl.multiple_of(j * LANES, LANES)
        x = x_ref[:, pl.ds(off, LANES)]
        for s in (1, 2, 4, 8, 16, 32, 64):
            x = x + jnp.where(lane >= s, pltpu.roll(x, s, 1), 0.0)
        x = x + carry
        o_ref[:, pl.ds(off, LANES)] = x
        return jnp.broadcast_to(x[:, LANES - 1:LANES], (rows, LANES))

    lax.fori_loop(0, length // LANES, body, jnp.zeros((rows, LANES), f32))


def _cumsum_lanes(x):
    rows, length = x.shape
    return pl.pallas_call(
        _cumsum_kernel,
        grid=(1,),
        in_specs=[pl.BlockSpec((rows, length), lambda i: (0, 0))],
        out_specs=pl.BlockSpec((rows, length), lambda i: (0, 0)),
        out_shape=jax.ShapeDtypeStruct((rows, length), f32),
        compiler_params=_cparams(("parallel",)),
        name="logf_cumsum",
    )(x)


def _dot_nt(a, b):
    return lax.dot_general(a, b, (((1,), (1,)), ((), ())), preferred_element_type=f32)


def _lane_halves(q):
    lane = lax.broadcasted_iota(i32, q.shape, 1)
    zero = jnp.zeros_like(q)
    return jnp.where(lane < HEAD_DIM, q, zero), jnp.where(lane >= HEAD_DIM, q, zero)


def _softmax_step(s, v_blk, hidx, m_sc, l_sc, acc_sc):
    m_old = m_sc[hidx]
    m_new = jnp.maximum(m_old, jnp.max(s, axis=1, keepdims=True))
    alpha = jnp.exp(m_old - m_new)
    p = jnp.exp(s - m_new)
    l_sc[hidx] = alpha * l_sc[hidx] + jnp.sum(p, axis=1, keepdims=True)
    acc_sc[hidx] = alpha * acc_sc[hidx] + jnp.dot(p.astype(bf16), v_blk, preferred_element_type=f32)
    m_sc[hidx] = m_new


def _init_softmax(m_sc, l_sc, acc_sc):
    m_sc[...] = jnp.full(m_sc.shape, NEG_INF, f32)
    l_sc[...] = jnp.zeros(l_sc.shape, f32)
    acc_sc[...] = jnp.zeros(acc_sc.shape, f32)


def _chunk_block_range(first_q, tq, tk, lk):
    last_q = first_q + tq - 1
    n_full = (((first_q >> CHUNK_SHIFT) + 1) * CHUNK) // tk
    n_full = jnp.minimum(n_full, lk // tk)
    n_kb = ((((last_q >> CHUNK_SHIFT) + 1) * CHUNK - 1) // tk) + 1
    n_kb = jnp.minimum(n_kb, (lk + tk - 1) // tk)
    return n_full, n_kb


def _chunk_valid(first_q, k0, tq, tk, lk):
    q_pos = first_q + lax.broadcasted_iota(i32, (tq, 1), 0)
    k_pos = k0 + lax.broadcasted_iota(i32, (1, tk), 1)
    return ((k_pos >> CHUNK_SHIFT) <= (q_pos >> CHUNK_SHIFT)) & (k_pos < lk)


def _order_key(x):
    b = lax.bitcast_convert_type(x, i32)
    return b ^ ((b >> 31) & jnp.int32(0x7FFFFFFF))


def _fox_kernel(q_ref, k_ref, v_ref, cq_ref, ck_ref, o_ref, m_sc, l_sc, acc_sc, *, tq, tk, past_len):
    qb = pl.program_id(2)
    first_q = past_len + qb * tq
    qh = _lane_halves(q_ref[0])
    cq = cq_ref[0, 0]
    _init_softmax(m_sc, l_sc, acc_sc)
    n_full = (first_q + 1) // tk
    n_kb = (first_q + tq - 1) // tk + 1

    def block(kb, masked):
        k0 = pl.multiple_of(kb * tk, tk)
        k_blk = k_ref[0, pl.ds(k0, tk), :]
        v_blk = v_ref[0, pl.ds(k0, tk), :]
        if masked:
            q_pos = first_q + lax.broadcasted_iota(i32, (tq, 1), 0)
            k_pos = k0 + lax.broadcasted_iota(i32, (1, tk), 1)
            visible = k_pos <= q_pos
        for hh in range(2):
            s = _dot_nt(qh[hh], k_blk)
            s = s + cq[:, hh:hh + 1] - ck_ref[0, 0, hh:hh + 1, pl.ds(k0, tk)]
            if masked:
                s = jnp.where(visible, s, NEG_INF)
            _softmax_step(s, v_blk, hh, m_sc, l_sc, acc_sc)

    def full_body(kb, c):
        block(kb, False)
        return c

    def masked_body(kb, c):
        block(kb, True)
        return c

    lax.fori_loop(0, n_full, full_body, 0)
    lax.fori_loop(n_full, n_kb, masked_body, 0)
    lane = lax.broadcasted_iota(i32, (tq, LANES), 1)
    o_ref[0] = jnp.where(lane < HEAD_DIM, acc_sc[0] / l_sc[0], acc_sc[1] / l_sc[1])


def _fox_attention(q16, k16, v16, cq4, ck4, past_len, tq, tk):
    b, lq, width = q16.shape
    lkp = k16.shape[1]
    ng = width // LANES
    return pl.pallas_call(
        functools.partial(_fox_kernel, tq=tq, tk=tk, past_len=past_len),
        grid=(b, ng, lq // tq),
        in_specs=[
            pl.BlockSpec((1, tq, LANES), lambda bi, g, qi: (bi, qi, g)),
            pl.BlockSpec((1, lkp, LANES), lambda bi, g, qi: (bi, 0, g)),
            pl.BlockSpec((1, lkp, LANES), lambda bi, g, qi: (bi, 0, g)),
            pl.BlockSpec((1, 1, tq, 2), lambda bi, g, qi: (bi, g, qi, 0)),
            pl.BlockSpec((1, 1, 2, lkp), lambda bi, g, qi: (bi, g, 0, 0)),
        ],
        out_specs=pl.BlockSpec((1, tq, LANES), lambda bi, g, qi: (bi, qi, g)),
        out_shape=jax.ShapeDtypeStruct((b, lq, width), f32),
        scratch_shapes=[
            pltpu.VMEM((2, tq, 1), f32),
            pltpu.VMEM((2, tq, 1), f32),
            pltpu.VMEM((2, tq, LANES), f32),
        ],
        compiler_params=_cparams(("parallel", "parallel", "arbitrary")),
        name="fox_attention",
    )(q16, k16, v16, cq4, ck4)


def _diff_kernel(q_ref, k_ref, v_ref, lp_ref, hg_ref, o_ref, m_sc, l_sc, acc_sc, *,
                 tq, tk, past_len, lk, lam_init):
    qb = pl.program_id(2)
    first_q = past_len + qb * tq
    qh = _lane_halves(q_ref[0])
    _init_softmax(m_sc, l_sc, acc_sc)
    n_full, n_kb = _chunk_block_range(first_q, tq, tk, lk)

    def block(kb, masked):
        k0 = pl.multiple_of(kb * tk, tk)
        k_blk = k_ref[0, pl.ds(k0, tk), :]
        v_blk = v_ref[0, pl.ds(k0, tk), :]
        if masked:
            valid = _chunk_valid(first_q, k0, tq, tk, lk)
        for c in range(2):
            s = _dot_nt(qh[c], k_blk)
            if masked:
                s = jnp.where(valid, s, NEG_INF)
            _softmax_step(s, v_blk, c, m_sc, l_sc, acc_sc)

    def full_body(kb, c):
        block(kb, False)
        return c

    def masked_body(kb, c):
        block(kb, True)
        return c

    lax.fori_loop(0, n_full, full_body, 0)
    lax.fori_loop(n_full, n_kb, masked_body, 0)
    o = acc_sc[0] / l_sc[0] - _diff_lambda(lp_ref, lam_init) * (acc_sc[1] / l_sc[1])
    o_ref[0] = _head_norm(o, hg_ref, lam_init)


def _diff_lambda(lp_ref, lam_init):
    lp = lp_ref[...]
    return (jnp.exp(jnp.sum(lp[0:1] * lp[1:2], axis=1, keepdims=True))
            - jnp.exp(jnp.sum(lp[2:3] * lp[3:4], axis=1, keepdims=True)) + lam_init)


def _head_norm(o, hg_ref, lam_init):
    ms = jnp.mean(o * o, axis=-1, keepdims=True)
    return (o * lax.rsqrt(ms + NORM_EPS) * hg_ref[...]) * (1.0 - lam_init)


def _diff_attention(q16, k16, v16, lam_params, head_gain, lam_init, past_len, lk, tq, tk):
    b, lq, width = q16.shape
    lkp = k16.shape[1]
    nh = width // LANES
    return pl.pallas_call(
        functools.partial(_diff_kernel, tq=tq, tk=tk, past_len=past_len, lk=lk, lam_init=lam_init),
        grid=(b, nh, lq // tq),
        in_specs=[
            pl.BlockSpec((1, tq, LANES), lambda bi, h, qi: (bi, qi, h)),
            pl.BlockSpec((1, lkp, LANES), lambda bi, h, qi: (bi, 0, h)),
            pl.BlockSpec((1, lkp, LANES), lambda bi, h, qi: (bi, 0, h)),
            pl.BlockSpec((4, HEAD_DIM), lambda bi, h, qi: (0, 0)),
            pl.BlockSpec((1, LANES), lambda bi, h, qi: (0, 0)),
        ],
        out_specs=pl.BlockSpec((1, tq, LANES), lambda bi, h, qi: (bi, qi, h)),
        out_shape=jax.ShapeDtypeStruct((b, lq, width), f32),
        scratch_shapes=[
            pltpu.VMEM((2, tq, 1), f32),
            pltpu.VMEM((2, tq, 1), f32),
            pltpu.VMEM((2, tq, LANES), f32),
        ],
        compiler_params=_cparams(("parallel", "parallel", "arbitrary")),
        name="diff_attention",
    )(q16, k16, v16, lam_params, head_gain)


def _dsa_kernel(q_ref, k_ref, v_ref, qi_ref, ki_ref, wi_ref, o_ref, key_sc, m_sc, l_sc, acc_sc, *,
                tq, tk, past_len, lk, topk, idx_bits):
    qb = pl.program_id(1)
    first_q = past_len + qb * tq
    n_full, n_kb = _chunk_block_range(first_q, tq, tk, lk)
    n_groups = tk // LANES

    def run_blocks(fn, init):
        c = lax.fori_loop(0, n_full, lambda kb, c: fn(kb, c, False), init)
        return lax.fori_loop(n_full, n_kb, lambda kb, c: fn(kb, c, True), c)

    qi_h = []
    for g in range(N_HEADS // 2):
        qi_h.extend(_lane_halves(qi_ref[0, :, g * LANES:(g + 1) * LANES]))
    wi = wi_ref[0] * (N_HEADS ** -0.5)

    def score_block(kb, c, masked):
        k0 = pl.multiple_of(kb * tk, tk)
        ki_blk = ki_ref[0, pl.ds(k0, tk), :]
        score = jnp.zeros((tq, tk), f32)
        for h in range(N_HEADS):
            d = _dot_nt(qi_h[h], ki_blk)
            score = score + jnp.maximum(d, 0.0) * wi[:, h:h + 1]
        if masked:
            score = jnp.where(_chunk_valid(first_q, k0, tq, tk, lk), score, NEG_INF)
        key_sc[:, pl.ds(k0, tk)] = _order_key(score)
        return c

    run_blocks(score_block, 0)

    def lane_fold(x):
        acc = x[:, 0:LANES]
        for g in range(1, n_groups):
            acc = acc + x[:, g * LANES:(g + 1) * LANES]
        return acc

    def count(indicator_fn):
        def body(kb, c):
            k0 = pl.multiple_of(kb * tk, tk)
            return c + lane_fold(indicator_fn(key_sc[:, pl.ds(k0, tk)], k0))
        c = lax.fori_loop(0, n_kb, body, jnp.zeros((tq, LANES), f32))
        return jnp.sum(c, axis=1, keepdims=True)

    def bit_body(i, thr):
        cand = thr ^ jnp.left_shift(jnp.int32(1), 31 - i)
        cnt = count(lambda key, k0: jnp.where(key >= cand, 1.0, 0.0))
        return jnp.where(cnt >= topk, cand, thr)

    thr = lax.fori_loop(0, 32, bit_body, jnp.full((tq, 1), INT_MIN, i32))

    cnt_gt = count(lambda key, k0: jnp.where(key > thr, 1.0, 0.0))
    cnt_eq = count(lambda key, k0: jnp.where(key == thr, 1.0, 0.0))
    need = topk - cnt_gt
    excess = jnp.max(jnp.where(cnt_eq > need, 1.0, 0.0)) > 0.0

    def tie_search():
        def tbit(i, last):
            cand = last | jnp.left_shift(jnp.int32(1), idx_bits - 1 - i)

            def indicator(key, k0):
                k_pos = k0 + lax.broadcasted_iota(i32, (1, tk), 1)
                return jnp.where(key == thr, jnp.where(k_pos < cand, 1.0, 0.0), 0.0)
            cnt = count(indicator)
            return jnp.where(cnt < need, cand, last)
        return lax.fori_loop(0, idx_bits, tbit, jnp.zeros((tq, 1), i32))

    tie_last = lax.cond(excess, tie_search, lambda: jnp.full((tq, 1), 2 ** idx_bits, i32))

    q_h = []
    for g in range(N_HEADS // 2):
        q_h.extend(_lane_halves(q_ref[0, :, g * LANES:(g + 1) * LANES]))
    _init_softmax(m_sc, l_sc, acc_sc)

    def attn_block(kb, c, masked):
        k0 = pl.multiple_of(kb * tk, tk)
        key = key_sc[:, pl.ds(k0, tk)]
        k_pos = k0 + lax.broadcasted_iota(i32, (1, tk), 1)
        tie_bias = jnp.where(k_pos <= tie_last, 0.0, NEG_INF)
        bias = jnp.where(key > thr, 0.0, jnp.where(key == thr, tie_bias, NEG_INF))
        if masked:
            bias = jnp.where(_chunk_valid(first_q, k0, tq, tk, lk), bias, NEG_INF)
        for h in range(N_HEADS):
            g = h // 2
            k_blk = k_ref[0, pl.ds(k0, tk), g * LANES:(g + 1) * LANES]
            v_blk = v_ref[0, pl.ds(k0, tk), g * LANES:(g + 1) * LANES]
            s = _dot_nt(q_h[h], k_blk) + bias
            _softmax_step(s, v_blk, h, m_sc, l_sc, acc_sc)
        return c

    run_blocks(attn_block, 0)
    lane = lax.broadcasted_iota(i32, (tq, LANES), 1)
    for g in range(N_HEADS // 2):
        o_ref[0, :, g * LANES:(g + 1) * LANES] = jnp.where(
            lane < HEAD_DIM, acc_sc[2 * g] / l_sc[2 * g], acc_sc[2 * g + 1] / l_sc[2 * g + 1])


def _dsa_attention(q16, k16, v16, qi16, ki2_16, wi, past_len, lk, tq, tk):
    b, lq, width = q16.shape
    lkp = k16.shape[1]
    topk = min(IDX_TOPK_MAX, lk // 4)
    idx_bits = max(1, int(lkp).bit_length())
    assert tk >= topk and lkp % tk == 0
    return pl.pallas_call(
        functools.partial(_dsa_kernel, tq=tq, tk=tk, past_len=past_len, lk=lk, topk=topk, idx_bits=idx_bits),
        grid=(b, lq // tq),
        in_specs=[
            pl.BlockSpec((1, tq, width), lambda bi, qi: (bi, qi, 0)),
            pl.BlockSpec((1, lkp, width), lambda bi, qi: (bi, 0, 0)),
            pl.BlockSpec((1, lkp, width), lambda bi, qi: (bi, 0, 0)),
            pl.BlockSpec((1, tq, width), lambda bi, qi: (bi, qi, 0)),
            pl.BlockSpec((1, lkp, LANES), lambda bi, qi: (bi, 0, 0)),
            pl.BlockSpec((1, tq, N_HEADS), lambda bi, qi: (bi, qi, 0)),
        ],
        out_specs=pl.BlockSpec((1, tq, width), lambda bi, qi: (bi, qi, 0)),
        out_shape=jax.ShapeDtypeStruct((b, lq, width), f32),
        scratch_shapes=[
            pltpu.VMEM((tq, lkp), i32),
            pltpu.VMEM((N_HEADS, tq, 1), f32),
            pltpu.VMEM((N_HEADS, tq, 1), f32),
            pltpu.VMEM((N_HEADS, tq, LANES), f32),
        ],
        compiler_params=_cparams(("parallel", "arbitrary")),
        name="dsa_attention",
    )(q16, k16, v16, qi16, ki2_16, wi)


def _resident_spec(block_shape, index_map):
    return pl.BlockSpec(block_shape, index_map, pipeline_mode=pl.Buffered(1))


def _tree_fold(x, op):
    slab = x.shape[0] // REDUCE_CHAINS
    parts = [x[i * slab:(i + 1) * slab] for i in range(REDUCE_CHAINS)]
    while len(parts) > 1:
        parts = [op(parts[i], parts[i + 1]) for i in range(0, len(parts), 2)]
    return parts[0]


def _softmax_step_t(s, v_t, hidx, m_sc, l_sc, acc_sc):
    m_old = m_sc[hidx]
    m_new = jnp.maximum(m_old, jnp.max(_tree_fold(s, jnp.maximum), axis=0, keepdims=True))
    alpha = jnp.exp(m_old - m_new)
    p = jnp.exp(s - m_new)
    l_sc[hidx] = alpha * l_sc[hidx] + jnp.sum(_tree_fold(p, jnp.add), axis=0, keepdims=True)
    acc_sc[hidx] = alpha * acc_sc[hidx] + jnp.dot(v_t, p.astype(bf16), preferred_element_type=f32)
    m_sc[hidx] = m_new


def _score_tile(buf, k_ref, lane0, k0, tk, q_t):
    half = tk // 2
    for i in range(2):
        buf[i * half:(i + 1) * half, :] = jnp.dot(
            k_ref[0, pl.ds(k0 + i * half, half), lane0:lane0 + LANES], q_t, preferred_element_type=f32)


def _score_pipeline(n_chains, n_kb, qk_fn, sm_fn, bufs):
    assert n_chains % 2 == 0
    qk_fn(0, 0, bufs[0])

    def body(kb, carry):
        for c in range(n_chains):
            if c + 1 < n_chains:
                qk_fn(kb, c + 1, bufs[(c + 1) % 2])
            else:
                qk_fn(kb + 1, 0, bufs[0])
            sm_fn(kb, c, bufs[c % 2], False)
        return carry

    lax.fori_loop(0, n_kb - 1, body, 0)
    for c in range(n_chains):
        if c + 1 < n_chains:
            qk_fn(n_kb - 1, c + 1, bufs[(c + 1) % 2])
        sm_fn(n_kb - 1, c, bufs[c % 2], True)


def _half_rows(x, upper):
    row = lax.broadcasted_iota(i32, x.shape, 0)
    keep = (row >= HEAD_DIM) if upper else (row < HEAD_DIM)
    return jnp.where(keep, x, jnp.zeros_like(x))


def _chunk_valid_t(first_q, k0, tq, tk, lk):
    k_pos = k0 + lax.broadcasted_iota(i32, (tk, tq), 0)
    q_pos = first_q + lax.broadcasted_iota(i32, (tk, tq), 1)
    return ((k_pos >> CHUNK_SHIFT) <= (q_pos >> CHUNK_SHIFT)) & (k_pos < lk)


def _store_head_pairs_t(o_ref, heads_t):
    for g in range(len(heads_t) // 2):
        o_t = jnp.concatenate([heads_t[2 * g], heads_t[2 * g + 1]], axis=0)
        o_ref[0, :, g * LANES:(g + 1) * LANES] = o_t.T


def _fox_t_kernel(q_ref, k_ref, v_ref, o_ref, m_sc, l_sc, acc_sc, sa_sc, sb_sc, *, tq, tk, past_len, nh):
    qb = pl.program_id(2)
    first_q = past_len + qb * tq
    n_kb = first_q // tk + 1
    _init_softmax(m_sc, l_sc, acc_sc)

    def qk(kb, c, buf):
        k0 = pl.multiple_of(kb * tk, tk)
        _score_tile(buf, k_ref, c * LANES, k0, tk, q_ref[0, c * LANES:(c + 1) * LANES, :])

    def sm(kb, c, buf, masked):
        k0 = pl.multiple_of(kb * tk, tk)
        s = buf[...]
        if masked:
            k_pos = k0 + lax.broadcasted_iota(i32, (tk, tq), 0)
            q_pos = first_q + lax.broadcasted_iota(i32, (tk, tq), 1)
            s = jnp.where(k_pos <= q_pos, s, NEG_INF)
        _softmax_step_t(s, v_ref[0, c * HEAD_DIM:(c + 1) * HEAD_DIM, pl.ds(k0, tk)], c, m_sc, l_sc, acc_sc)

    _score_pipeline(nh, n_kb, qk, sm, (sa_sc, sb_sc))
    _store_head_pairs_t(o_ref, [acc_sc[c] / l_sc[c] for c in range(nh)])


def _fox_attention_t(q_aug_t, k_aug, v_t, past_len, tq, tk, nh):
    b, _, lq = q_aug_t.shape
    lkp = k_aug.shape[1]
    assert tk % tq == 0 and past_len % tk == 0 and lkp % tk == 0
    return pl.pallas_call(
        functools.partial(_fox_t_kernel, tq=tq, tk=tk, past_len=past_len, nh=nh),
        grid=(b, N_HEADS // nh, lq // tq),
        in_specs=[
            pl.BlockSpec((1, nh * LANES, tq), lambda bi, g, qi: (bi, g, qi)),
            _resident_spec((1, lkp, nh * LANES), lambda bi, g, qi: (bi, 0, g)),
            _resident_spec((1, nh * HEAD_DIM, lkp), lambda bi, g, qi: (bi, g, 0)),
        ],
        out_specs=pl.BlockSpec((1, tq, nh * HEAD_DIM), lambda bi, g, qi: (bi, qi, g)),
        out_shape=jax.ShapeDtypeStruct((b, lq, N_HEADS * HEAD_DIM), f32),
        scratch_shapes=[
            pltpu.VMEM((nh, 1, tq), f32),
            pltpu.VMEM((nh, 1, tq), f32),
            pltpu.VMEM((nh, HEAD_DIM, tq), f32),
            pltpu.VMEM((tk, tq), f32),
            pltpu.VMEM((tk, tq), f32),
        ],
        compiler_params=_cparams(("parallel", "parallel", "arbitrary")),
        name="fox_attention_t",
    )(q_aug_t, k_aug, v_t)


def _bf16_prefix(x):
    bits = lax.bitcast_convert_type(x, jnp.uint32) & jnp.uint32(0xFFFF0000)
    return lax.bitcast_convert_type(bits, f32)


def _split3(x):
    x1 = _bf16_prefix(x)
    r = x - x1
    x2 = _bf16_prefix(r)
    return x1.astype(bf16), x2.astype(bf16), (r - x2).astype(bf16)


def _fox_operands_t(aq16, ak16, av16, cum, past_len):
    b, lq, _ = aq16.shape
    lk = ak16.shape[1]
    pad = LANES - HEAD_DIM - FOX_BIAS_COLS
    ck = [jnp.swapaxes(c, 1, 2)[..., None] for c in _split3(cum)]
    one_k = jnp.ones((b, lk, N_HEADS, 1), bf16)
    k_aug = jnp.concatenate(
        [ak16.reshape(b, lk, N_HEADS, HEAD_DIM), one_k, one_k, one_k, -ck[0], -ck[1], -ck[2],
         jnp.zeros((b, lk, N_HEADS, pad), bf16)], axis=-1).reshape(b, lk, N_HEADS * LANES)
    cq = [c[:, :, None, :] for c in _split3(cum[:, :, past_len:past_len + lq])]
    one_q = jnp.ones((b, N_HEADS, 1, lq), bf16)
    q_t = jnp.transpose(aq16.reshape(b, lq, N_HEADS, HEAD_DIM), (0, 2, 3, 1))
    q_aug_t = jnp.concatenate(
        [q_t, cq[0], cq[1], cq[2], one_q, one_q, one_q, jnp.zeros((b, N_HEADS, pad, lq), bf16)],
        axis=2).reshape(b, N_HEADS * LANES, lq)
    return q_aug_t, k_aug, jnp.swapaxes(av16, 1, 2)


def _diff_t_kernel(q_ref, k_ref, v_ref, lp_ref, hg_ref, o_ref, m_sc, l_sc, acc_sc, sa_sc, sb_sc, *,
                   tq, tk, past_len, lk, lam_init, nh):
    qb = pl.program_id(2)
    first_q = past_len + qb * tq
    n_kb = first_q // tk + 1
    _init_softmax(m_sc, l_sc, acc_sc)

    def qk(kb, c, buf):
        k0 = pl.multiple_of(kb * tk, tk)
        h = c // 2
        q_c = _half_rows(q_ref[0, h * LANES:(h + 1) * LANES, :], c % 2 == 1)
        _score_tile(buf, k_ref, h * LANES, k0, tk, q_c)

    def sm(kb, c, buf, masked):
        k0 = pl.multiple_of(kb * tk, tk)
        h = c // 2
        s = buf[...]
        if masked:
            s = jnp.where(_chunk_valid_t(first_q, k0, tq, tk, lk), s, NEG_INF)
        _softmax_step_t(s, v_ref[0, h * LANES:(h + 1) * LANES, pl.ds(k0, tk)], c, m_sc, l_sc, acc_sc)

    _score_pipeline(2 * nh, n_kb, qk, sm, (sa_sc, sb_sc))
    lam = _diff_lambda(lp_ref, lam_init)
    for h in range(nh):
        o_t = acc_sc[2 * h] / l_sc[2 * h] - lam * (acc_sc[2 * h + 1] / l_sc[2 * h + 1])
        o_ref[0, :, h * LANES:(h + 1) * LANES] = _head_norm(o_t.T, hg_ref, lam_init)


def _diff_attention_t(q_t, k16, v_t, lam_params, head_gain, lam_init, past_len, lk, tq, tk, nh):
    b, width, lq = q_t.shape
    lkp = k16.shape[1]
    n_heads = width // LANES
    assert tk % tq == 0 and past_len % tk == 0 and lkp % tk == 0 and tq % CHUNK == 0
    return pl.pallas_call(
        functools.partial(_diff_t_kernel, tq=tq, tk=tk, past_len=past_len, lk=lk, lam_init=lam_init, nh=nh),
        grid=(b, n_heads // nh, lq // tq),
        in_specs=[
            pl.BlockSpec((1, nh * LANES, tq), lambda bi, g, qi: (bi, g, qi)),
            _resident_spec((1, lkp, nh * LANES), lambda bi, g, qi: (bi, 0, g)),
            _resident_spec((1, nh * LANES, lkp), lambda bi, g, qi: (bi, g, 0)),
            pl.BlockSpec((4, HEAD_DIM), lambda bi, g, qi: (0, 0)),
            pl.BlockSpec((1, LANES), lambda bi, g, qi: (0, 0)),
        ],
        out_specs=pl.BlockSpec((1, tq, nh * LANES), lambda bi, g, qi: (bi, qi, g)),
        out_shape=jax.ShapeDtypeStruct((b, lq, width), f32),
        scratch_shapes=[
            pltpu.VMEM((2 * nh, 1, tq), f32),
            pltpu.VMEM((2 * nh, 1, tq), f32),
            pltpu.VMEM((2 * nh, LANES, tq), f32),
            pltpu.VMEM((tk, tq), f32),
            pltpu.VMEM((tk, tq), f32),
        ],
        compiler_params=_cparams(("parallel", "parallel", "arbitrary")),
        name="diff_attention_t",
    )(q_t, k16, v_t, lam_params, head_gain)


def _fold_rows(x, rows_out, op):
    n = x.shape[0] // rows_out
    chains = min(FOLD_CHAINS, n)
    accs = [x[j * rows_out:(j + 1) * rows_out] for j in range(chains)]
    for i in range(chains, n):
        accs[i % chains] = op(accs[i % chains], x[i * rows_out:(i + 1) * rows_out])
    while len(accs) > 1:
        accs = [op(accs[i], accs[i + 1]) for i in range(0, len(accs), 2)]
    return accs[0]


FOLD_CHAINS = 4
HALF_BIAS = 32768
PACKED_ROWS = 16


def _dsa_t_kernel(q_ref, k_ref, v_ref, qi_ref, ki_ref, wi_ref, o_ref, hi_sc, lo_sc, bias_sc, tri_sc, tie_room_sc,
                  m_sc, l_sc, acc_sc, sa_sc, sb_sc, *, tq, tk, past_len, lk, topk):
    qb = pl.program_id(1)
    first_q = past_len + qb * tq
    n_kb = first_q // tk + 1

    def head_rows(ref, h):
        g = h // 2
        return _half_rows(ref[0, g * LANES:(g + 1) * LANES, :], h % 2 == 1)

    wi = wi_ref[0] * (N_HEADS ** -0.5)

    def score_block(kb, masked):
        k0 = pl.multiple_of(kb * tk, tk)
        ki_blk = ki_ref[0, pl.ds(k0, tk), :]
        score = jnp.zeros((tk, tq), f32)
        for h in range(N_HEADS):
            d = jnp.dot(ki_blk, head_rows(qi_ref, h), preferred_element_type=f32)
            score = score + jnp.maximum(d, 0.0) * wi[h:h + 1, :]
        if masked:
            score = jnp.where(_chunk_valid_t(first_q, k0, tq, tk, lk), score, NEG_INF)
        key = _order_key(score)
        hi_sc[pl.ds(k0, tk), :] = (key >> 16).astype(jnp.int16)
        lo_sc[pl.ds(k0, tk), :] = ((key & 0xFFFF) - HALF_BIAS).astype(jnp.int16)

    def score_body(kb, c):
        score_block(kb, False)
        return c

    lax.fori_loop(0, n_kb - 1, score_body, 0)
    score_block(n_kb - 1, True)

    one16, zero16 = jnp.int16(1), jnp.int16(0)

    def count16(indicator_fn):
        def body(kb, c):
            k0 = pl.multiple_of(kb * tk, tk)
            ind = indicator_fn(hi_sc[pl.ds(k0, tk), :], lo_sc[pl.ds(k0, tk), :])
            return c + _fold_rows(ind, PACKED_ROWS, jnp.add)
        c = lax.fori_loop(0, n_kb, body, jnp.zeros((PACKED_ROWS, tq), jnp.int16))
        return jnp.sum(c.astype(i32), axis=0, keepdims=True)

    def search16(use_lo, need):
        def bit_body(i, t_off):
            cand = t_off | jnp.left_shift(jnp.int32(1), 15 - i)
            c16 = (cand - HALF_BIAS).astype(jnp.int16)
            cnt = count16(lambda hi, lo: jnp.where((lo if use_lo else hi) >= c16, one16, zero16))
            return jnp.where(cnt >= need, cand, t_off)
        return lax.fori_loop(0, 16, bit_body, jnp.zeros((1, tq), i32))

    thi = search16(False, topk) - HALF_BIAS
    thi16 = thi.astype(jnp.int16)
    cnt_hi_gt = count16(lambda hi, lo: jnp.where(hi > thi16, one16, zero16))

    def keep_lo_body(kb, c):
        k0 = pl.multiple_of(kb * tk, tk)
        lo_sc[pl.ds(k0, tk), :] = jnp.where(hi_sc[pl.ds(k0, tk), :] == thi16, lo_sc[pl.ds(k0, tk), :],
                                            jnp.int16(-HALF_BIAS))
        return c

    lax.fori_loop(0, n_kb, keep_lo_body, 0)
    tlo16 = (search16(True, topk - cnt_hi_gt) - HALF_BIAS).astype(jnp.int16)
    cnt_gt = cnt_hi_gt + count16(lambda hi, lo: jnp.where(lo > tlo16, one16, zero16))

    _init_softmax(m_sc, l_sc, acc_sc)
    tie_room_sc[...] = (topk - cnt_gt).astype(f32)
    tri_sc[...] = jnp.where(lax.broadcasted_iota(i32, (tk, tk), 1) <= lax.broadcasted_iota(i32, (tk, tk), 0),
                            1.0, 0.0).astype(bf16)

    def qk(kb, h, buf):
        k0 = pl.multiple_of(kb * tk, tk)
        _score_tile(buf, k_ref, (h // 2) * LANES, k0, tk, head_rows(q_ref, h))

    def sm(kb, h, buf, masked):
        k0 = pl.multiple_of(kb * tk, tk)
        if h == 0:
            hi, lo = hi_sc[pl.ds(k0, tk), :], lo_sc[pl.ds(k0, tk), :]
            two_b, one_b, zero_b = jnp.bfloat16(2.0), jnp.bfloat16(1.0), jnp.bfloat16(0.0)
            low_code = jnp.where(lo > tlo16, two_b, jnp.where(lo == tlo16, one_b, zero_b))
            code_b = jnp.where(hi > thi16, two_b, jnp.where(hi == thi16, low_code, zero_b))
            tied = jnp.where(code_b == one_b, one_b, zero_b)
            rank = jnp.dot(tri_sc[...], tied, preferred_element_type=f32)
            room = tie_room_sc[...]
            x = code_b.astype(f32) - jnp.where(rank <= room, 0.0, 1.0)
            bias = jnp.where(x >= 1.0, 0.0, NEG_INF)
            if masked:
                bias = jnp.where(_chunk_valid_t(first_q, k0, tq, tk, lk), bias, NEG_INF)
            bias_sc[...] = bias
            tie_room_sc[...] = room - rank[tk - 1:tk, :]
        s = buf[...] + bias_sc[...]
        _softmax_step_t(s, v_ref[0, h * HEAD_DIM:(h + 1) * HEAD_DIM, pl.ds(k0, tk)], h, m_sc, l_sc, acc_sc)

    _score_pipeline(N_HEADS, n_kb, qk, sm, (sa_sc, sb_sc))
    _store_head_pairs_t(o_ref, [acc_sc[h] / l_sc[h] for h in range(N_HEADS)])


def _dsa_attention_t(q_t, k16, v_t, qi_t, ki2_16, wi_t, past_len, lk, tq, tk):
    b, width, lq = q_t.shape
    lkp = k16.shape[1]
    topk = min(IDX_TOPK_MAX, lk // 4)
    assert tk >= topk and lkp % tk == 0 and tk % tq == 0 and past_len % tk == 0 and tq % CHUNK == 0
    return pl.pallas_call(
        functools.partial(_dsa_t_kernel, tq=tq, tk=tk, past_len=past_len, lk=lk, topk=topk),
        grid=(b, lq // tq),
        in_specs=[
            pl.BlockSpec((1, width, tq), lambda bi, qi: (bi, 0, qi)),
            _resident_spec((1, lkp, width), lambda bi, qi: (bi, 0, 0)),
            _resident_spec((1, width, lkp), lambda bi, qi: (bi, 0, 0)),
            pl.BlockSpec((1, width, tq), lambda bi, qi: (bi, 0, qi)),
            _resident_spec((1, lkp, LANES), lambda bi, qi: (bi, 0, 0)),
            pl.BlockSpec((1, N_HEADS, tq), lambda bi, qi: (bi, 0, qi)),
        ],
        out_specs=pl.BlockSpec((1, tq, width), lambda bi, qi: (bi, qi, 0)),
        out_shape=jax.ShapeDtypeStruct((b, lq, width), f32),
        scratch_shapes=[
            pltpu.VMEM((lkp, tq), jnp.int16),
            pltpu.VMEM((lkp, tq), jnp.int16),
            pltpu.VMEM((tk, tq), f32),
            pltpu.VMEM((tk, tk), bf16),
            pltpu.VMEM((1, tq), f32),
            pltpu.VMEM((N_HEADS, 1, tq), f32),
            pltpu.VMEM((N_HEADS, 1, tq), f32),
            pltpu.VMEM((N_HEADS, HEAD_DIM, tq), f32),
            pltpu.VMEM((tk, tq), f32),
            pltpu.VMEM((tk, tq), f32),
        ],
        compiler_params=_cparams(("parallel", "arbitrary")),
        name="dsa_attention_t",
    )(q_t, k16, v_t, qi_t, ki2_16, wi_t)


def _out_kernel(n_parts, final, *refs):
    parts = refs[:2 * n_parts]
    w_ref, x_ref = refs[2 * n_parts], refs[2 * n_parts + 1]
    rest = refs[2 * n_parts + 2:]
    acc = x_ref[...]
    row = 0
    for i in range(n_parts):
        o, g = parts[2 * i][...], parts[2 * i + 1][...]
        mixed = (o * (g / (1.0 + jnp.exp(-g)))).astype(bf16)
        width = mixed.shape[1]
        acc = acc + jnp.dot(mixed, w_ref[row:row + width, :], preferred_element_type=f32)
        row += width
    if final:
        fg_ref, xo_ref, y_ref = rest
        ms = jnp.mean(acc * acc, axis=-1, keepdims=True)
        y_ref[...] = acc * lax.rsqrt(ms + NORM_EPS) * fg_ref[...]
    else:
        (xo_ref,) = rest
    xo_ref[...] = acc


def _out_proj(parts, w16, x2d, final_gain, tm):
    n, d = x2d.shape
    n_parts = len(parts)
    final = final_gain is not None
    in_specs, args = [], []
    for (o, g) in parts:
        for a in (o, g):
            in_specs.append(pl.BlockSpec((tm, a.shape[1]), lambda i: (i, 0)))
            args.append(a)
    in_specs += [pl.BlockSpec(w16.shape, lambda i: (0, 0)), pl.BlockSpec((tm, d), lambda i: (i, 0))]
    args += [w16, x2d]
    out_shape = [jax.ShapeDtypeStruct((n, d), f32)]
    out_specs = [pl.BlockSpec((tm, d), lambda i: (i, 0))]
    if final:
        in_specs.append(pl.BlockSpec((1, d), lambda i: (0, 0)))
        args.append(final_gain)
        out_shape.append(jax.ShapeDtypeStruct((n, d), f32))
        out_specs.append(pl.BlockSpec((tm, d), lambda i: (i, 0)))
    res = pl.pallas_call(
        functools.partial(_out_kernel, n_parts, final),
        grid=(n // tm,),
        in_specs=in_specs,
        out_specs=out_specs,
        out_shape=out_shape,
        compiler_params=_cparams(("parallel",)),
        name="gated_out_proj",
    )(*args)
    return res if final else (res[0], None)


KEY_MAJOR_TILE = 512
FOX_HEADS_PER_STEP = 4
DIFF_HEADS_PER_STEP = 2


def _row_tile(n):
    return 256 if n % 256 == 0 else n


def _key_major(lq, lk, past_len):
    return lq % KEY_MAJOR_TILE == 0 and lk % KEY_MAJOR_TILE == 0 and past_len % KEY_MAJOR_TILE == 0


def _padded_keys(lq, lk, past_len):
    return lk if _key_major(lq, lk, past_len) else -(-lk // LANES) * LANES


def _with_cache16(cache, new16, lkp):
    b, lq, width = new16.shape
    if cache is None:
        return new16
    past = cache.shape[1]
    pad = jnp.zeros((b, lkp - past - lq, width), bf16)
    return jnp.concatenate([cache.reshape(b, past, width).astype(bf16), new16, pad], axis=1)


def _even_layer(x, gain, w_in16, w_out16, b_forget, past_len, cache, final_gain):
    b, lq, d = x.shape
    n = b * lq
    lk = past_len + lq
    lkp = _padded_keys(lq, lk, past_len)
    tm = _row_tile(n)
    pos = past_len + jnp.arange(lq)
    cos, sa, sb = (jnp.tile(t, (b, 1)) for t in _rope_tables(pos))
    bias = jnp.zeros((1, LANES), f32).at[0, HEAD_DIM:HEAD_DIM + N_HEADS].set(b_forget.astype(f32))
    (aq16, ak, ak16, av, av16, ag, bq16, bk, bk16, bv, bv16, bg, qi16, misc) = _proj(
        x.reshape(n, d), gain.reshape(1, d), w_in16, cos, sa, sb, bias, EVEN_PLAN, tm)
    ki = misc[:, :HEAD_DIM]
    logf = misc[:, HEAD_DIM:HEAD_DIM + N_HEADS]
    wi = misc[:, HEAD_DIM + N_HEADS:HEAD_DIM + 2 * N_HEADS]

    def r3(a):
        return a.reshape(b, lq, a.shape[-1])

    c_ak, c_av, c_af, c_bk, c_bv, c_bki = cache if cache is not None else (None,) * 6
    ki16 = r3(ki).astype(bf16)
    logf3 = r3(logf)
    if cache is not None:
        logf_all = jnp.concatenate([c_af.astype(f32), logf3], axis=1)
        logf_all = jnp.pad(logf_all, ((0, 0), (0, lkp - lk), (0, 0)))
    else:
        logf_all = logf3
    cum = _cumsum_lanes(jnp.swapaxes(logf_all, 1, 2).reshape(b * N_HEADS, lkp)).reshape(b, N_HEADS, lkp)

    ak_all, av_all = _with_cache16(c_ak, r3(ak16), lkp), _with_cache16(c_av, r3(av16), lkp)
    bk_all, bv_all = _with_cache16(c_bk, r3(bk16), lkp), _with_cache16(c_bv, r3(bv16), lkp)
    ki_all = _with_cache16(c_bki, ki16, lkp)
    ki2 = jnp.concatenate([ki_all, ki_all], axis=-1)
    if _key_major(lq, lk, past_len):
        q_aug_t, k_aug, v_t = _fox_operands_t(r3(aq16), ak_all, av_all, cum, past_len)
        a_out = _fox_attention_t(q_aug_t, k_aug, v_t, past_len, KEY_MAJOR_TILE, KEY_MAJOR_TILE, FOX_HEADS_PER_STEP)
        b_out = _dsa_attention_t(jnp.swapaxes(r3(bq16), 1, 2), bk_all, jnp.swapaxes(bv_all, 1, 2),
                                 jnp.swapaxes(r3(qi16), 1, 2), ki2, jnp.swapaxes(r3(wi), 1, 2),
                                 past_len, lk, KEY_MAJOR_TILE, KEY_MAJOR_TILE)
    else:
        ck4 = cum.reshape(b, N_HEADS // 2, 2, lkp)
        cq4 = jnp.swapaxes(cum[:, :, past_len:past_len + lq].reshape(b, N_HEADS // 2, 2, lq), 2, 3)
        a_out = _fox_attention(r3(aq16), ak_all, av_all, cq4, ck4, past_len, lq, lkp)
        b_out = _dsa_attention(r3(bq16), bk_all, bv_all, r3(qi16), ki2, r3(wi), past_len, lk, lq, lkp)
    x_new, y = _out_proj([(a_out.reshape(n, -1), ag), (b_out.reshape(n, -1), bg)], w_out16, x.reshape(n, d),
                         final_gain, tm)
    state = (ak.reshape(b, lq, N_HEADS, HEAD_DIM), av.reshape(b, lq, N_HEADS, HEAD_DIM), logf3,
             bk.reshape(b, lq, N_HEADS, HEAD_DIM), bv.reshape(b, lq, N_HEADS, HEAD_DIM), r3(ki))
    return x_new.reshape(b, lq, d), (None if y is None else y.reshape(b, lq, d)), state


def _odd_layer(x, gain, w_in16, lam_params, head_gain, w_out16, lam_init, past_len, cache, final_gain):
    b, lq, d = x.shape
    n = b * lq
    lk = past_len + lq
    lkp = _padded_keys(lq, lk, past_len)
    tm = _row_tile(n)
    pos = past_len + jnp.arange(lq)
    cos, sa, sb = (jnp.tile(t, (b, 1)) for t in _rope_tables(pos))
    bias = jnp.zeros((1, LANES), f32)
    q16, k, k16, v, v16, g = _proj(x.reshape(n, d), gain.reshape(1, d), w_in16, cos, sa, sb, bias, ODD_PLAN, tm)

    def r3(a):
        return a.reshape(b, lq, a.shape[-1])

    c_ck, c_cv = cache if cache is not None else (None, None)
    k_all, v_all = _with_cache16(c_ck, r3(k16), lkp), _with_cache16(c_cv, r3(v16), lkp)
    lam_args = (lam_params.astype(f32), head_gain.reshape(1, LANES).astype(f32), lam_init, past_len, lk)
    if _key_major(lq, lk, past_len):
        o = _diff_attention_t(jnp.swapaxes(r3(q16), 1, 2), k_all, jnp.swapaxes(v_all, 1, 2), *lam_args,
                              KEY_MAJOR_TILE, KEY_MAJOR_TILE, DIFF_HEADS_PER_STEP)
    else:
        o = _diff_attention(r3(q16), k_all, v_all, *lam_args, lq, lkp)
    x_new, y = _out_proj([(o.reshape(n, -1), g)], w_out16, x.reshape(n, d), final_gain, tm)
    state = (k.reshape(b, lq, N_HEADS, 2 * HEAD_DIM), v.reshape(b, lq, N_HEADS, 2 * HEAD_DIM))
    return x_new.reshape(b, lq, d), (None if y is None else y.reshape(b, lq, d)), state


def kernel(x_prompt, x_sample, cache_a_k, cache_a_v, cache_a_logf, cache_b_k, cache_b_v, cache_b_kidx, cache_c_k, cache_c_v, norm_gain, final_gain, w_in_even, b_forget, w_out_even, w_in_odd, lambda_params, c_head_gain, w_out_odd):
    past_len = cache_a_k.shape[2]
    depth = norm_gain.shape[0]
    xp, xs = x_prompt, x_sample
    yp = ys = None
    even_p, even_s, odd_p, odd_s = [], [], [], []
    fg = final_gain.reshape(1, -1).astype(f32)
    for layer in range(depth):
        g = norm_gain[layer]
        i = layer // 2
        last = fg if layer == depth - 1 else None
        if layer % 2 == 0:
            w_in16 = _reorder_even_weight(w_in_even[i])
            w_out16 = w_out_even[i].astype(bf16)
            xp, yp, st_p = _even_layer(xp, g, w_in16, w_out16, b_forget[i], 0, None, last)
            xs, ys, st_s = _even_layer(xs, g, w_in16, w_out16, b_forget[i], past_len,
                                       (cache_a_k[i], cache_a_v[i], cache_a_logf[i],
                                        cache_b_k[i], cache_b_v[i], cache_b_kidx[i]), last)
            even_p.append(st_p)
            even_s.append(st_s)
        else:
            lam_init = 0.8 - 0.6 * math.exp(-0.3 * layer)
            w_in16 = w_in_odd[i].astype(bf16)
            w_out16 = w_out_odd[i].astype(bf16)
            xp, yp, st_p = _odd_layer(xp, g, w_in16, lambda_params[i], c_head_gain[i], w_out16, lam_init,
                                      0, None, last)
            xs, ys, st_s = _odd_layer(xs, g, w_in16, lambda_params[i], c_head_gain[i], w_out16, lam_init,
                                      past_len, (cache_c_k[i], cache_c_v[i]), last)
            odd_p.append(st_p)
            odd_s.append(st_s)
    a_k_p, a_v_p, a_f_p, b_k_p, b_v_p, b_i_p = [jnp.stack(t) for t in zip(*even_p)]
    a_k_s, a_v_s, a_f_s, b_k_s, b_v_s, b_i_s = [jnp.stack(t) for t in zip(*even_s)]
    c_k_p, c_v_p = [jnp.stack(t) for t in zip(*odd_p)]
    c_k_s, c_v_s = [jnp.stack(t) for t in zip(*odd_s)]
    return (yp, ys,
            a_k_p, a_v_p, a_f_p, b_k_p, b_v_p, b_i_p, c_k_p, c_v_p,
            a_k_s, a_v_s, a_f_s, b_k_s, b_v_s, b_i_s, c_k_s, c_v_s)
```

```python
import functools
import math

import jax
import jax.numpy as jnp
from jax import lax
from jax.experimental import pallas as pl
from jax.experimental.pallas import tpu as pltpu

LANES = 128
HEAD_DIM = 64
CHUNK = 64
CHUNK_SHIFT = 6
ROT_DIM = HEAD_DIM // 4
ROPE_THETA = 500000.0
IDX_TOPK_MAX = 256
N_HEADS = 8
NORM_EPS = 1e-6
NEG_INF = -1e30
QK_SCALE = HEAD_DIM ** -0.5
LOG2E = 1.4426950408889634
Q_SCALE = QK_SCALE * LOG2E
VALUE_PAD = 16
NEW_PAD = LANES
INT_MIN = -2 ** 31
VMEM_LIMIT = 56 * 1024 * 1024
REDUCE_CHAINS = 8
FOX_BIAS_COLS = 6

f32 = jnp.float32
bf16 = jnp.bfloat16
i32 = jnp.int32


def _cparams(sem):
    return pltpu.CompilerParams(dimension_semantics=sem, vmem_limit_bytes=VMEM_LIMIT)


def _rope_group(r, cos, sa, sb):
    return r * cos + pltpu.roll(r, 8, 1) * sa + pltpu.roll(r, LANES - 8, 1) * sb


def _proj_kernel(plan, x_ref, g_ref, w_ref, cos_ref, sa_ref, sb_ref, bias_ref, *out_refs):
    x = x_ref[...]
    ms = jnp.mean(x * x, axis=-1, keepdims=True)
    h = (x * lax.rsqrt(ms + NORM_EPS) * g_ref[...]).astype(bf16)
    cos, sa, sb = cos_ref[...], sa_ref[...], sb_ref[...]
    lane = lax.broadcasted_iota(i32, (1, LANES), 1)
    oi = 0
    for (c0, width, kind, outs) in plan:
        r = jnp.dot(h, w_ref[:, c0:c0 + width], preferred_element_type=f32)
        if kind == "rope":
            r = jnp.concatenate(
                [_rope_group(r[:, g * LANES:(g + 1) * LANES], cos, sa, sb) for g in range(width // LANES)],
                axis=1)
        elif kind == "misc":
            low = lane < HEAD_DIM
            roped = _rope_group(r, jnp.where(low, cos, 1.0), jnp.where(low, sa, 0.0), jnp.where(low, sb, 0.0))
            z = r + bias_ref[...]
            logf = jnp.minimum(z, 0.0) - jnp.log(1.0 + jnp.exp(-jnp.abs(z)))
            r = jnp.where((lane >= HEAD_DIM) & (lane < HEAD_DIM + N_HEADS), logf, roped)
        for (dtype, scale) in outs:
            val = r if scale == 1.0 else r * scale
            out_refs[oi][...] = val.astype(dtype)
            oi += 1


def _proj(x2d, gain, w16, cos, sa, sb, bias, plan, tm):
    n, d = x2d.shape
    wtot = w16.shape[1]
    out_shapes, out_specs = [], []
    for (c0, width, kind, outs) in plan:
        for (dtype, _) in outs:
            out_shapes.append(jax.ShapeDtypeStruct((n, width), dtype))
            out_specs.append(pl.BlockSpec((tm, width), lambda i: (i, 0)))
    return pl.pallas_call(
        functools.partial(_proj_kernel, plan),
        grid=(n // tm,),
        in_specs=[
            pl.BlockSpec((tm, d), lambda i: (i, 0)),
            pl.BlockSpec((1, d), lambda i: (0, 0)),
            pl.BlockSpec((d, wtot), lambda i: (0, 0)),
            pl.BlockSpec((tm, LANES), lambda i: (i, 0)),
            pl.BlockSpec((tm, LANES), lambda i: (i, 0)),
            pl.BlockSpec((tm, LANES), lambda i: (i, 0)),
            pl.BlockSpec((1, LANES), lambda i: (0, 0)),
        ],
        out_specs=out_specs,
        out_shape=out_shapes,
        compiler_params=_cparams(("parallel",)),
        name="norm_in_proj",
    )(x2d, gain, w16, cos, sa, sb, bias)


EVEN_PLAN = (
    (0, 512, "plain", ((bf16, Q_SCALE),)),
    (512, 512, "plain", ((f32, 1.0), (bf16, 1.0))),
    (1024, 512, "plain", ((f32, 1.0), (bf16, 1.0))),
    (1536, 512, "plain", ((f32, 1.0),)),
    (2048, 512, "rope", ((bf16, Q_SCALE),)),
    (2560, 512, "rope", ((f32, 1.0), (bf16, 1.0))),
    (3072, 512, "plain", ((f32, 1.0), (bf16, 1.0))),
    (3584, 512, "plain", ((f32, 1.0),)),
    (4096, 512, "rope", ((bf16, QK_SCALE),)),
    (4608, LANES, "misc", ((f32, 1.0),)),
)
ODD_PLAN = (
    (0, 1024, "rope", ((bf16, Q_SCALE),)),
    (1024, 1024, "rope", ((f32, 1.0), (bf16, 1.0))),
    (2048, 1024, "plain", ((f32, 1.0), (bf16, 1.0))),
    (3072, 1024, "plain", ((f32, 1.0),)),
)


def _reorder_even_weight(w):
    a4 = w[:, 0:2048]
    af = w[:, 2048:2056]
    b4 = w[:, 2056:4104]
    qi = w[:, 4104:4616]
    ki = w[:, 4616:4680]
    wi = w[:, 4680:4688]
    pad = jnp.zeros((w.shape[0], LANES - HEAD_DIM - 2 * N_HEADS), w.dtype)
    return jnp.concatenate([a4, b4, qi, ki, af, wi, pad], axis=1).astype(bf16)


def _rope_tables(pos):
    half = ROT_DIM // 2
    inv = ROPE_THETA ** (-jnp.arange(half, dtype=f32) * 2.0 / ROT_DIM)
    ang = pos.astype(f32)[:, None] * inv[None, :]
    cos, sin = jnp.cos(ang), jnp.sin(ang)
    n = pos.shape[0]
    rest = HEAD_DIM - ROT_DIM
    cos64 = jnp.concatenate([cos, cos, jnp.ones((n, rest), f32)], axis=1)
    sa64 = jnp.concatenate([jnp.zeros((n, half), f32), sin, jnp.zeros((n, rest), f32)], axis=1)
    sb64 = jnp.concatenate([-sin, jnp.zeros((n, half + rest), f32)], axis=1)
    return tuple(jnp.concatenate([t, t], axis=1) for t in (cos64, sa64, sb64))


def _cumsum_kernel(x_ref, o_ref):
    rows, length = x_ref.shape
    lane = lax.broadcasted_iota(i32, (rows, LANES), 1)

    def body(j, carry):
        off = pl.multiple_of(j * LANES, LANES)
        x = x_ref[:, pl.ds(off, LANES)]
        for s in (1, 2, 4, 8, 16, 32, 64):
            x = x + jnp.where(lane >= s, pltpu.roll(x, s, 1), 0.0)
        x = x + carry
        o_ref[:, pl.ds(off, LANES)] = x
        return jnp.broadcast_to(x[:, LANES - 1:LANES], (rows, LANES))

    lax.fori_loop(0, length // LANES, body, jnp.zeros((rows, LANES), f32))


def _cumsum_lanes(x):
    rows, length = x.shape
    return pl.pallas_call(
        _cumsum_kernel,
        grid=(1,),
        in_specs=[pl.BlockSpec((rows, length), lambda i: (0, 0))],
        out_specs=pl.BlockSpec((rows, length), lambda i: (0, 0)),
        out_shape=jax.ShapeDtypeStruct((rows, length), f32),
        compiler_params=_cparams(("parallel",)),
        name="logf_cumsum",
    )(x)


def _dot_nt(a, b):
    return lax.dot_general(a, b, (((1,), (1,)), ((), ())), preferred_element_type=f32)


def _lane_halves(q):
    lane = lax.broadcasted_iota(i32, q.shape, 1)
    zero = jnp.zeros_like(q)
    return jnp.where(lane < HEAD_DIM, q, zero), jnp.where(lane >= HEAD_DIM, q, zero)


def _softmax_step(s, v_blk, hidx, m_sc, l_sc, acc_sc):
    m_old = m_sc[hidx]
    m_new = jnp.maximum(m_old, jnp.max(s, axis=1, keepdims=True))
    alpha = jnp.exp2(m_old - m_new)
    p = jnp.exp2(s - m_new)
    l_sc[hidx] = alpha * l_sc[hidx] + jnp.sum(p, axis=1, keepdims=True)
    acc_sc[hidx] = alpha * acc_sc[hidx] + jnp.dot(p.astype(bf16), v_blk, preferred_element_type=f32)
    m_sc[hidx] = m_new


def _init_softmax(m_sc, l_sc, acc_sc):
    m_sc[...] = jnp.full(m_sc.shape, NEG_INF, f32)
    l_sc[...] = jnp.zeros(l_sc.shape, f32)
    acc_sc[...] = jnp.zeros(acc_sc.shape, f32)


def _order_key(x):
    b = lax.bitcast_convert_type(x, i32)
    return b ^ ((b >> 31) & jnp.int32(0x7FFFFFFF))


def _new_row_positions(lq):
    return lax.broadcasted_iota(i32, (lq, 1), 0), lax.broadcasted_iota(i32, (1, NEW_PAD), 1)


def _fox_cached_kernel(q_ref, kc_ref, vc_ref, kn_ref, vn_ref, cq_ref, ckc_ref, ckn_ref, o_ref, m_sc, l_sc, acc_sc, *,
                       lq):
    qh = _lane_halves(q_ref[0])
    cq = cq_ref[0, 0]
    _init_softmax(m_sc, l_sc, acc_sc)
    kc, vc = kc_ref[0].astype(bf16), vc_ref[0].astype(bf16)
    kn, vn = kn_ref[0], vn_ref[0]
    q_idx, k_idx = _new_row_positions(lq)
    visible = k_idx <= q_idx
    for hh in range(2):
        s = _dot_nt(qh[hh], kc) + cq[:, hh:hh + 1] - ckc_ref[0, 0, hh:hh + 1, :]
        _softmax_step(s, vc, hh, m_sc, l_sc, acc_sc)
        s = _dot_nt(qh[hh], kn) + cq[:, hh:hh + 1] - ckn_ref[0, 0, hh:hh + 1, :]
        _softmax_step(jnp.where(visible, s, NEG_INF), vn, hh, m_sc, l_sc, acc_sc)
    lane = lax.broadcasted_iota(i32, (lq, LANES), 1)
    o_ref[0] = jnp.where(lane < HEAD_DIM, acc_sc[0] / l_sc[0], acc_sc[1] / l_sc[1])


def _softmax_scratch(n, lq):
    return [pltpu.VMEM((n, lq, 1), f32), pltpu.VMEM((n, lq, 1), f32), pltpu.VMEM((n, lq, LANES), f32)]


def _fox_attention_cached(q16, kc, vc, kn16, vn16, cq4, ckc4, ckn4):
    b, lq, width = q16.shape
    past = kc.shape[1]
    ng = width // LANES
    return pl.pallas_call(
        functools.partial(_fox_cached_kernel, lq=lq),
        grid=(b, ng),
        in_specs=[
            pl.BlockSpec((1, lq, LANES), lambda bi, g: (bi, 0, g)),
            pl.BlockSpec((1, past, LANES), lambda bi, g: (bi, 0, g)),
            pl.BlockSpec((1, past, LANES), lambda bi, g: (bi, 0, g)),
            pl.BlockSpec((1, NEW_PAD, LANES), lambda bi, g: (bi, 0, g)),
            pl.BlockSpec((1, NEW_PAD, LANES), lambda bi, g: (bi, 0, g)),
            pl.BlockSpec((1, 1, lq, 2), lambda bi, g: (bi, g, 0, 0)),
            pl.BlockSpec((1, 1, 2, past), lambda bi, g: (bi, g, 0, 0)),
            pl.BlockSpec((1, 1, 2, NEW_PAD), lambda bi, g: (bi, g, 0, 0)),
        ],
        out_specs=pl.BlockSpec((1, lq, LANES), lambda bi, g: (bi, 0, g)),
        out_shape=jax.ShapeDtypeStruct((b, lq, width), f32),
        scratch_shapes=_softmax_scratch(2, lq),
        compiler_params=_cparams(("parallel", "parallel")),
        name="fox_attention_cached",
    )(q16, kc, vc, kn16, vn16, cq4, ckc4, ckn4)


def _diff_lambda(lp_ref, lam_init):
    lp = lp_ref[...]
    return (jnp.exp(jnp.sum(lp[0:1] * lp[1:2], axis=1, keepdims=True))
            - jnp.exp(jnp.sum(lp[2:3] * lp[3:4], axis=1, keepdims=True)) + lam_init)


def _head_norm(o, hg_ref, lam_init):
    ms = jnp.mean(o * o, axis=-1, keepdims=True)
    return (o * lax.rsqrt(ms + NORM_EPS) * hg_ref[...]) * (1.0 - lam_init)


def _new_chunk_valid(lq, past_len):
    q_idx, k_idx = _new_row_positions(lq)
    return (((past_len + k_idx) >> CHUNK_SHIFT) <= ((past_len + q_idx) >> CHUNK_SHIFT)) & (k_idx < lq)


def _diff_cached_kernel(q_ref, kc_ref, vc_ref, kn_ref, vn_ref, lp_ref, hg_ref, o_ref, m_sc, l_sc, acc_sc, *,
                        lq, past_len, lam_init):
    qh = _lane_halves(q_ref[0])
    _init_softmax(m_sc, l_sc, acc_sc)
    kc, vc = kc_ref[0].astype(bf16), vc_ref[0].astype(bf16)
    kn, vn = kn_ref[0], vn_ref[0]
    valid = _new_chunk_valid(lq, past_len)
    for c in range(2):
        _softmax_step(_dot_nt(qh[c], kc), vc, c, m_sc, l_sc, acc_sc)
        _softmax_step(jnp.where(valid, _dot_nt(qh[c], kn), NEG_INF), vn, c, m_sc, l_sc, acc_sc)
    o = acc_sc[0] / l_sc[0] - _diff_lambda(lp_ref, lam_init) * (acc_sc[1] / l_sc[1])
    o_ref[0] = _head_norm(o, hg_ref, lam_init)


def _diff_attention_cached(q16, kc, vc, kn16, vn16, lam_params, head_gain, lam_init, past_len):
    b, lq, width = q16.shape
    past = kc.shape[1]
    nh = width // LANES
    return pl.pallas_call(
        functools.partial(_diff_cached_kernel, lq=lq, past_len=past_len, lam_init=lam_init),
        grid=(b, nh),
        in_specs=[
            pl.BlockSpec((1, lq, LANES), lambda bi, h: (bi, 0, h)),
            pl.BlockSpec((1, past, LANES), lambda bi, h: (bi, 0, h)),
            pl.BlockSpec((1, past, LANES), lambda bi, h: (bi, 0, h)),
            pl.BlockSpec((1, NEW_PAD, LANES), lambda bi, h: (bi, 0, h)),
            pl.BlockSpec((1, NEW_PAD, LANES), lambda bi, h: (bi, 0, h)),
            pl.BlockSpec((4, HEAD_DIM), lambda bi, h: (0, 0)),
            pl.BlockSpec((1, LANES), lambda bi, h: (0, 0)),
        ],
        out_specs=pl.BlockSpec((1, lq, LANES), lambda bi, h: (bi, 0, h)),
        out_shape=jax.ShapeDtypeStruct((b, lq, width), f32),
        scratch_shapes=_softmax_scratch(2, lq),
        compiler_params=_cparams(("parallel", "parallel")),
        name="diff_attention_cached",
    )(q16, kc, vc, kn16, vn16, lam_params, head_gain)


def _dsa_cached_kernel(q_ref, kc_ref, vc_ref, kn_ref, vn_ref, qi_ref, kic_ref, kin_ref, wi_ref, o_ref,
                       key_sc, m_sc, l_sc, acc_sc, *, lq, past, past_len, topk, idx_bits):
    n_keys = past + NEW_PAD
    n_groups = n_keys // LANES
    valid_new = _new_chunk_valid(lq, past_len)

    qi_h = []
    for g in range(N_HEADS // 2):
        qi_h.extend(_lane_halves(qi_ref[0, :, g * LANES:(g + 1) * LANES]))
    wi = wi_ref[0] * (N_HEADS ** -0.5)

    def scores(ki_blk):
        score = jnp.zeros((lq, ki_blk.shape[0]), f32)
        for h in range(N_HEADS):
            score = score + jnp.maximum(_dot_nt(qi_h[h], ki_blk), 0.0) * wi[:, h:h + 1]
        return score

    key_sc[:, 0:past] = _order_key(scores(kic_ref[0]))
    key_sc[:, past:n_keys] = _order_key(jnp.where(valid_new, scores(kin_ref[0]), NEG_INF))

    def count(indicator):
        acc = indicator[:, 0:LANES]
        for g in range(1, n_groups):
            acc = acc + indicator[:, g * LANES:(g + 1) * LANES]
        return jnp.sum(acc, axis=1, keepdims=True)

    def bit_body(i, thr):
        cand = thr ^ jnp.left_shift(jnp.int32(1), 31 - i)
        cnt = count(jnp.where(key_sc[...] >= cand, 1.0, 0.0))
        return jnp.where(cnt >= topk, cand, thr)

    thr = lax.fori_loop(0, 32, bit_body, jnp.full((lq, 1), INT_MIN, i32))

    key = key_sc[...]
    k_pos = lax.broadcasted_iota(i32, (1, n_keys), 1)
    need = topk - count(jnp.where(key > thr, 1.0, 0.0))
    excess = jnp.max(jnp.where(count(jnp.where(key == thr, 1.0, 0.0)) > need, 1.0, 0.0)) > 0.0

    def tie_search():
        def tbit(i, last):
            cand = last | jnp.left_shift(jnp.int32(1), idx_bits - 1 - i)
            cnt = count(jnp.where(key_sc[...] == thr, jnp.where(k_pos < cand, 1.0, 0.0), 0.0))
            return jnp.where(cnt < need, cand, last)
        return lax.fori_loop(0, idx_bits, tbit, jnp.zeros((lq, 1), i32))

    tie_last = lax.cond(excess, tie_search, lambda: jnp.full((lq, 1), 2 ** idx_bits, i32))

    tie_bias = jnp.where(k_pos <= tie_last, 0.0, NEG_INF)
    bias = jnp.where(key > thr, 0.0, jnp.where(key == thr, tie_bias, NEG_INF))
    bias_c = bias[:, 0:past]
    bias_n = jnp.where(valid_new, bias[:, past:n_keys], NEG_INF)
    _init_softmax(m_sc, l_sc, acc_sc)
    lane = lax.broadcasted_iota(i32, (lq, LANES), 1)
    for g in range(N_HEADS // 2):
        cols = slice(g * LANES, (g + 1) * LANES)
        q_pair = _lane_halves(q_ref[0, :, cols])
        kc, vc = kc_ref[0, :, cols].astype(bf16), vc_ref[0, :, cols].astype(bf16)
        kn, vn = kn_ref[0, :, cols], vn_ref[0, :, cols]
        for hh in range(2):
            h = 2 * g + hh
            _softmax_step(_dot_nt(q_pair[hh], kc) + bias_c, vc, h, m_sc, l_sc, acc_sc)
            _softmax_step(_dot_nt(q_pair[hh], kn) + bias_n, vn, h, m_sc, l_sc, acc_sc)
        o_ref[0, :, cols] = jnp.where(lane < HEAD_DIM, acc_sc[2 * g] / l_sc[2 * g],
                                      acc_sc[2 * g + 1] / l_sc[2 * g + 1])


def _dsa_attention_cached(q16, kc, vc, kn16, vn16, qi16, kic2, kin2, wi, past_len):
    b, lq, width = q16.shape
    past = kc.shape[1]
    topk = min(IDX_TOPK_MAX, (past + lq) // 4)
    idx_bits = max(1, int(past + NEW_PAD).bit_length())
    assert past >= topk and past % LANES == 0
    return pl.pallas_call(
        functools.partial(_dsa_cached_kernel, lq=lq, past=past, past_len=past_len, topk=topk, idx_bits=idx_bits),
        grid=(b,),
        in_specs=[
            pl.BlockSpec((1, lq, width), lambda bi: (bi, 0, 0)),
            pl.BlockSpec((1, past, width), lambda bi: (bi, 0, 0)),
            pl.BlockSpec((1, past, width), lambda bi: (bi, 0, 0)),
            pl.BlockSpec((1, NEW_PAD, width), lambda bi: (bi, 0, 0)),
            pl.BlockSpec((1, NEW_PAD, width), lambda bi: (bi, 0, 0)),
            pl.BlockSpec((1, lq, width), lambda bi: (bi, 0, 0)),
            pl.BlockSpec((1, past, LANES), lambda bi: (bi, 0, 0)),
            pl.BlockSpec((1, NEW_PAD, LANES), lambda bi: (bi, 0, 0)),
            pl.BlockSpec((1, lq, N_HEADS), lambda bi: (bi, 0, 0)),
        ],
        out_specs=pl.BlockSpec((1, lq, width), lambda bi: (bi, 0, 0)),
        out_shape=jax.ShapeDtypeStruct((b, lq, width), f32),
        scratch_shapes=[pltpu.VMEM((lq, past + NEW_PAD), i32)] + _softmax_scratch(N_HEADS, lq),
        compiler_params=_cparams(("parallel",)),
        name="dsa_attention_cached",
    )(q16, kc, vc, kn16, vn16, qi16, kic2, kin2, wi)


def _resident_spec(block_shape, index_map):
    return pl.BlockSpec(block_shape, index_map, pipeline_mode=pl.Buffered(1))


def _tree_fold(x, op):
    slab = x.shape[0] // REDUCE_CHAINS
    parts = [x[i * slab:(i + 1) * slab] for i in range(REDUCE_CHAINS)]
    while len(parts) > 1:
        parts = [op(parts[i], parts[i + 1]) for i in range(0, len(parts), 2)]
    return parts[0]


def _softmax_step_t(s, v_aug, hidx, m_sc, acc_sc):
    m_old = m_sc[hidx]
    m_new = jnp.maximum(m_old, jnp.max(_tree_fold(s, jnp.maximum), axis=0, keepdims=True))
    alpha = jnp.exp2(m_old - m_new)
    p = jnp.exp2(s - m_new)
    acc_sc[hidx] = alpha * acc_sc[hidx] + jnp.dot(v_aug, p.astype(bf16), preferred_element_type=f32)
    m_sc[hidx] = m_new


def _init_softmax_t(m_sc, acc_sc):
    m_sc[...] = jnp.full(m_sc.shape, NEG_INF, f32)
    acc_sc[...] = jnp.zeros(acc_sc.shape, f32)


def _normalized_t(acc_sc, hidx, dv):
    acc = acc_sc[hidx]
    return acc[:dv] / acc[dv:dv + 1]


def _values_aug_t(v16, dv):
    b, lk, width = v16.shape
    nh = width // dv
    v_t = jnp.swapaxes(v16, 1, 2).reshape(b, nh, dv, lk)
    ones = jnp.ones((b, nh, 1, lk), bf16)
    zeros = jnp.zeros((b, nh, VALUE_PAD - 1, lk), bf16)
    return jnp.concatenate([v_t, ones, zeros], axis=2).reshape(b, nh * (dv + VALUE_PAD), lk)


def _score_tile(buf, k_ref, lane0, k0, tk, q_t):
    half = tk // 2
    for i in range(2):
        buf[i * half:(i + 1) * half, :] = jnp.dot(
            k_ref[0, pl.ds(k0 + i * half, half), lane0:lane0 + LANES], q_t, preferred_element_type=f32)


def _score_pipeline(n_chains, n_kb, qk_fn, sm_fn, bufs):
    assert n_chains % 2 == 0
    qk_fn(0, 0, bufs[0])

    def body(kb, carry):
        for c in range(n_chains):
            if c + 1 < n_chains:
                qk_fn(kb, c + 1, bufs[(c + 1) % 2])
            else:
                qk_fn(kb + 1, 0, bufs[0])
            sm_fn(kb, c, bufs[c % 2], False)
        return carry

    lax.fori_loop(0, n_kb - 1, body, 0)
    for c in range(n_chains):
        if c + 1 < n_chains:
            qk_fn(n_kb - 1, c + 1, bufs[(c + 1) % 2])
        sm_fn(n_kb - 1, c, bufs[c % 2], True)


def _half_rows(x, upper):
    row = lax.broadcasted_iota(i32, x.shape, 0)
    keep = (row >= HEAD_DIM) if upper else (row < HEAD_DIM)
    return jnp.where(keep, x, jnp.zeros_like(x))


def _chunk_valid_t(first_q, k0, tq, tk, lk):
    k_pos = k0 + lax.broadcasted_iota(i32, (tk, tq), 0)
    q_pos = first_q + lax.broadcasted_iota(i32, (tk, tq), 1)
    return ((k_pos >> CHUNK_SHIFT) <= (q_pos >> CHUNK_SHIFT)) & (k_pos < lk)


def _store_head_pairs_t(o_ref, heads_t):
    for g in range(len(heads_t) // 2):
        o_t = jnp.concatenate([heads_t[2 * g], heads_t[2 * g + 1]], axis=0)
        o_ref[0, :, g * LANES:(g + 1) * LANES] = o_t.T


def _fox_t_kernel(q_ref, k_ref, v_ref, o_ref, m_sc, acc_sc, sa_sc, sb_sc, *, tq, tk, past_len, nh):
    qb = pl.program_id(2)
    first_q = past_len + qb * tq
    n_kb = first_q // tk + 1
    dva = HEAD_DIM + VALUE_PAD
    _init_softmax_t(m_sc, acc_sc)

    def qk(kb, c, buf):
        k0 = pl.multiple_of(kb * tk, tk)
        _score_tile(buf, k_ref, c * LANES, k0, tk, q_ref[0, c * LANES:(c + 1) * LANES, :])

    def sm(kb, c, buf, masked):
        k0 = pl.multiple_of(kb * tk, tk)
        s = buf[...]
        if masked:
            k_pos = k0 + lax.broadcasted_iota(i32, (tk, tq), 0)
            q_pos = first_q + lax.broadcasted_iota(i32, (tk, tq), 1)
            s = jnp.where(k_pos <= q_pos, s, NEG_INF)
        _softmax_step_t(s, v_ref[0, c * dva:(c + 1) * dva, pl.ds(k0, tk)], c, m_sc, acc_sc)

    _score_pipeline(nh, n_kb, qk, sm, (sa_sc, sb_sc))
    _store_head_pairs_t(o_ref, [_normalized_t(acc_sc, c, HEAD_DIM) for c in range(nh)])


def _fox_attention_t(q_aug_t, k_aug, v_aug_t, past_len, tq, tk, nh):
    b, _, lq = q_aug_t.shape
    lkp = k_aug.shape[1]
    assert tk % tq == 0 and past_len % tk == 0 and lkp % tk == 0
    return pl.pallas_call(
        functools.partial(_fox_t_kernel, tq=tq, tk=tk, past_len=past_len, nh=nh),
        grid=(b, N_HEADS // nh, lq // tq),
        in_specs=[
            pl.BlockSpec((1, nh * LANES, tq), lambda bi, g, qi: (bi, g, qi)),
            _resident_spec((1, lkp, nh * LANES), lambda bi, g, qi: (bi, 0, g)),
            _resident_spec((1, nh * (HEAD_DIM + VALUE_PAD), lkp), lambda bi, g, qi: (bi, g, 0)),
        ],
        out_specs=pl.BlockSpec((1, tq, nh * HEAD_DIM), lambda bi, g, qi: (bi, qi, g)),
        out_shape=jax.ShapeDtypeStruct((b, lq, N_HEADS * HEAD_DIM), f32),
        scratch_shapes=[
            pltpu.VMEM((nh, 1, tq), f32),
            pltpu.VMEM((nh, HEAD_DIM + VALUE_PAD, tq), f32),
            pltpu.VMEM((tk, tq), f32),
            pltpu.VMEM((tk, tq), f32),
        ],
        compiler_params=_cparams(("parallel", "parallel", "arbitrary")),
        name="fox_attention_t",
    )(q_aug_t, k_aug, v_aug_t)


def _bf16_prefix(x):
    bits = lax.bitcast_convert_type(x, jnp.uint32) & jnp.uint32(0xFFFF0000)
    return lax.bitcast_convert_type(bits, f32)


def _split3(x):
    x1 = _bf16_prefix(x)
    r = x - x1
    x2 = _bf16_prefix(r)
    return x1.astype(bf16), x2.astype(bf16), (r - x2).astype(bf16)


def _fox_operands_t(aq16, ak16, av16, cum, past_len):
    b, lq, _ = aq16.shape
    lk = ak16.shape[1]
    pad = LANES - HEAD_DIM - FOX_BIAS_COLS
    ck = [jnp.swapaxes(c, 1, 2)[..., None] for c in _split3(cum)]
    one_k = jnp.ones((b, lk, N_HEADS, 1), bf16)
    k_aug = jnp.concatenate(
        [ak16.reshape(b, lk, N_HEADS, HEAD_DIM), one_k, one_k, one_k, -ck[0], -ck[1], -ck[2],
         jnp.zeros((b, lk, N_HEADS, pad), bf16)], axis=-1).reshape(b, lk, N_HEADS * LANES)
    cq = [c[:, :, None, :] for c in _split3(cum[:, :, past_len:past_len + lq])]
    one_q = jnp.ones((b, N_HEADS, 1, lq), bf16)
    q_t = jnp.transpose(aq16.reshape(b, lq, N_HEADS, HEAD_DIM), (0, 2, 3, 1))
    q_aug_t = jnp.concatenate(
        [q_t, cq[0], cq[1], cq[2], one_q, one_q, one_q, jnp.zeros((b, N_HEADS, pad, lq), bf16)],
        axis=2).reshape(b, N_HEADS * LANES, lq)
    return q_aug_t, k_aug, _values_aug_t(av16, HEAD_DIM)


def _diff_t_kernel(q_ref, k_ref, v_ref, lp_ref, hg_ref, o_ref, m_sc, acc_sc, sa_sc, sb_sc, *,
                   tq, tk, past_len, lk, lam_init, nh):
    qb = pl.program_id(2)
    first_q = past_len + qb * tq
    n_kb = first_q // tk + 1
    dva = LANES + VALUE_PAD
    _init_softmax_t(m_sc, acc_sc)

    def qk(kb, c, buf):
        k0 = pl.multiple_of(kb * tk, tk)
        h = c // 2
        q_c = _half_rows(q_ref[0, h * LANES:(h + 1) * LANES, :], c % 2 == 1)
        _score_tile(buf, k_ref, h * LANES, k0, tk, q_c)

    def sm(kb, c, buf, masked):
        k0 = pl.multiple_of(kb * tk, tk)
        h = c // 2
        s = buf[...]
        if masked:
            s = jnp.where(_chunk_valid_t(first_q, k0, tq, tk, lk), s, NEG_INF)
        _softmax_step_t(s, v_ref[0, h * dva:(h + 1) * dva, pl.ds(k0, tk)], c, m_sc, acc_sc)

    _score_pipeline(2 * nh, n_kb, qk, sm, (sa_sc, sb_sc))
    lam = _diff_lambda(lp_ref, lam_init)
    for h in range(nh):
        o_t = _normalized_t(acc_sc, 2 * h, LANES) - lam * _normalized_t(acc_sc, 2 * h + 1, LANES)
        o_ref[0, :, h * LANES:(h + 1) * LANES] = _head_norm(o_t.T, hg_ref, lam_init)


def _diff_attention_t(q_t, k16, v_t, lam_params, head_gain, lam_init, past_len, lk, tq, tk, nh):
    b, width, lq = q_t.shape
    lkp = k16.shape[1]
    n_heads = width // LANES
    assert tk % tq == 0 and past_len % tk == 0 and lkp % tk == 0 and tq % CHUNK == 0
    return pl.pallas_call(
        functools.partial(_diff_t_kernel, tq=tq, tk=tk, past_len=past_len, lk=lk, lam_init=lam_init, nh=nh),
        grid=(b, n_heads // nh, lq // tq),
        in_specs=[
            pl.BlockSpec((1, nh * LANES, tq), lambda bi, g, qi: (bi, g, qi)),
            _resident_spec((1, lkp, nh * LANES), lambda bi, g, qi: (bi, 0, g)),
            _resident_spec((1, nh * (LANES + VALUE_PAD), lkp), lambda bi, g, qi: (bi, g, 0)),
            pl.BlockSpec((4, HEAD_DIM), lambda bi, g, qi: (0, 0)),
            pl.BlockSpec((1, LANES), lambda bi, g, qi: (0, 0)),
        ],
        out_specs=pl.BlockSpec((1, tq, nh * LANES), lambda bi, g, qi: (bi, qi, g)),
        out_shape=jax.ShapeDtypeStruct((b, lq, width), f32),
        scratch_shapes=[
            pltpu.VMEM((2 * nh, 1, tq), f32),
            pltpu.VMEM((2 * nh, LANES + VALUE_PAD, tq), f32),
            pltpu.VMEM((tk, tq), f32),
            pltpu.VMEM((tk, tq), f32),
        ],
        compiler_params=_cparams(("parallel", "parallel", "arbitrary")),
        name="diff_attention_t",
    )(q_t, k16, v_t, lam_params, head_gain)


def _fold_rows(x, rows_out, op):
    n = x.shape[0] // rows_out
    chains = min(FOLD_CHAINS, n)
    accs = [x[j * rows_out:(j + 1) * rows_out] for j in range(chains)]
    for i in range(chains, n):
        accs[i % chains] = op(accs[i % chains], x[i * rows_out:(i + 1) * rows_out])
    while len(accs) > 1:
        accs = [op(accs[i], accs[i + 1]) for i in range(0, len(accs), 2)]
    return accs[0]


FOLD_CHAINS = 4
HALF_BIAS = 32768
PACKED_ROWS = 16


def _dsa_t_kernel(q_ref, k_ref, v_ref, qi_ref, ki_ref, wi_ref, o_ref, hi_sc, lo_sc, bias_sc, tri_sc, tie_room_sc,
                  m_sc, acc_sc, sa_sc, sb_sc, *, tq, tk, past_len, lk, topk):
    qb = pl.program_id(1)
    first_q = past_len + qb * tq
    n_kb = first_q // tk + 1

    def head_rows(ref, h):
        g = h // 2
        return _half_rows(ref[0, g * LANES:(g + 1) * LANES, :], h % 2 == 1)

    wi = wi_ref[0] * (N_HEADS ** -0.5)

    def score_block(kb, masked):
        k0 = pl.multiple_of(kb * tk, tk)
        ki_blk = ki_ref[0, pl.ds(k0, tk), :]
        score = jnp.zeros((tk, tq), f32)
        for h in range(N_HEADS):
            d = jnp.dot(ki_blk, head_rows(qi_ref, h), preferred_element_type=f32)
            score = score + jnp.maximum(d, 0.0) * wi[h:h + 1, :]
        if masked:
            score = jnp.where(_chunk_valid_t(first_q, k0, tq, tk, lk), score, NEG_INF)
        key = _order_key(score)
        hi_sc[pl.ds(k0, tk), :] = (key >> 16).astype(jnp.int16)
        lo_sc[pl.ds(k0, tk), :] = ((key & 0xFFFF) - HALF_BIAS).astype(jnp.int16)

    def score_body(kb, c):
        score_block(kb, False)
        return c

    lax.fori_loop(0, n_kb - 1, score_body, 0)
    score_block(n_kb - 1, True)

    one16, zero16 = jnp.int16(1), jnp.int16(0)

    def count16(indicator_fn):
        def body(kb, c):
            k0 = pl.multiple_of(kb * tk, tk)
            ind = indicator_fn(hi_sc[pl.ds(k0, tk), :], lo_sc[pl.ds(k0, tk), :])
            return c + _fold_rows(ind, PACKED_ROWS, jnp.add)
        c = lax.fori_loop(0, n_kb, body, jnp.zeros((PACKED_ROWS, tq), jnp.int16))
        return jnp.sum(c.astype(i32), axis=0, keepdims=True)

    def search16(use_lo, need):
        def bit_body(i, t_off):
            cand = t_off | jnp.left_shift(jnp.int32(1), 15 - i)
            c16 = (cand - HALF_BIAS).astype(jnp.int16)
            cnt = count16(lambda hi, lo: jnp.where((lo if use_lo else hi) >= c16, one16, zero16))
            return jnp.where(cnt >= need, cand, t_off)
        return lax.fori_loop(0, 16, bit_body, jnp.zeros((1, tq), i32))

    thi = search16(False, topk) - HALF_BIAS
    thi16 = thi.astype(jnp.int16)
    cnt_hi_gt = count16(lambda hi, lo: jnp.where(hi > thi16, one16, zero16))

    def keep_lo_body(kb, c):
        k0 = pl.multiple_of(kb * tk, tk)
        lo_sc[pl.ds(k0, tk), :] = jnp.where(hi_sc[pl.ds(k0, tk), :] == thi16, lo_sc[pl.ds(k0, tk), :],
                                            jnp.int16(-HALF_BIAS))
        return c

    lax.fori_loop(0, n_kb, keep_lo_body, 0)
    tlo16 = (search16(True, topk - cnt_hi_gt) - HALF_BIAS).astype(jnp.int16)
    cnt_gt = cnt_hi_gt + count16(lambda hi, lo: jnp.where(lo > tlo16, one16, zero16))

    dva = HEAD_DIM + VALUE_PAD
    _init_softmax_t(m_sc, acc_sc)
    tie_room_sc[...] = (topk - cnt_gt).astype(f32)
    tri_sc[...] = jnp.where(lax.broadcasted_iota(i32, (tk, tk), 1) <= lax.broadcasted_iota(i32, (tk, tk), 0),
                            1.0, 0.0).astype(bf16)

    def qk(kb, h, buf):
        k0 = pl.multiple_of(kb * tk, tk)
        _score_tile(buf, k_ref, (h // 2) * LANES, k0, tk, head_rows(q_ref, h))

    def sm(kb, h, buf, masked):
        k0 = pl.multiple_of(kb * tk, tk)
        if h == 0:
            hi, lo = hi_sc[pl.ds(k0, tk), :], lo_sc[pl.ds(k0, tk), :]
            two_b, one_b, zero_b = jnp.bfloat16(2.0), jnp.bfloat16(1.0), jnp.bfloat16(0.0)
            low_code = jnp.where(lo > tlo16, two_b, jnp.where(lo == tlo16, one_b, zero_b))
            code_b = jnp.where(hi > thi16, two_b, jnp.where(hi == thi16, low_code, zero_b))
            tied = jnp.where(code_b == one_b, one_b, zero_b)
            rank = jnp.dot(tri_sc[...], tied, preferred_element_type=f32)
            room = tie_room_sc[...]
            x = code_b.astype(f32) - jnp.where(rank <= room, 0.0, 1.0)
            bias = jnp.where(x >= 1.0, 0.0, NEG_INF)
            if masked:
                bias = jnp.where(_chunk_valid_t(first_q, k0, tq, tk, lk), bias, NEG_INF)
            bias_sc[...] = bias
            tie_room_sc[...] = room - rank[tk - 1:tk, :]
        s = buf[...] + bias_sc[...]
        _softmax_step_t(s, v_ref[0, h * dva:(h + 1) * dva, pl.ds(k0, tk)], h, m_sc, acc_sc)

    _score_pipeline(N_HEADS, n_kb, qk, sm, (sa_sc, sb_sc))
    _store_head_pairs_t(o_ref, [_normalized_t(acc_sc, h, HEAD_DIM) for h in range(N_HEADS)])


def _dsa_attention_t(q_t, k16, v_t, qi_t, ki2_16, wi_t, past_len, lk, tq, tk):
    b, width, lq = q_t.shape
    lkp = k16.shape[1]
    topk = min(IDX_TOPK_MAX, lk // 4)
    assert tk >= topk and lkp % tk == 0 and tk % tq == 0 and past_len % tk == 0 and tq % CHUNK == 0
    return pl.pallas_call(
        functools.partial(_dsa_t_kernel, tq=tq, tk=tk, past_len=past_len, lk=lk, topk=topk),
        grid=(b, lq // tq),
        in_specs=[
            pl.BlockSpec((1, width, tq), lambda bi, qi: (bi, 0, qi)),
            _resident_spec((1, lkp, width), lambda bi, qi: (bi, 0, 0)),
            _resident_spec((1, N_HEADS * (HEAD_DIM + VALUE_PAD), lkp), lambda bi, qi: (bi, 0, 0)),
            pl.BlockSpec((1, width, tq), lambda bi, qi: (bi, 0, qi)),
            _resident_spec((1, lkp, LANES), lambda bi, qi: (bi, 0, 0)),
            pl.BlockSpec((1, N_HEADS, tq), lambda bi, qi: (bi, 0, qi)),
        ],
        out_specs=pl.BlockSpec((1, tq, width), lambda bi, qi: (bi, qi, 0)),
        out_shape=jax.ShapeDtypeStruct((b, lq, width), f32),
        scratch_shapes=[
            pltpu.VMEM((lkp, tq), jnp.int16),
            pltpu.VMEM((lkp, tq), jnp.int16),
            pltpu.VMEM((tk, tq), f32),
            pltpu.VMEM((tk, tk), bf16),
            pltpu.VMEM((1, tq), f32),
            pltpu.VMEM((N_HEADS, 1, tq), f32),
            pltpu.VMEM((N_HEADS, HEAD_DIM + VALUE_PAD, tq), f32),
            pltpu.VMEM((tk, tq), f32),
            pltpu.VMEM((tk, tq), f32),
        ],
        compiler_params=_cparams(("parallel", "arbitrary")),
        name="dsa_attention_t",
    )(q_t, k16, v_t, qi_t, ki2_16, wi_t)


def _out_kernel(n_parts, final, *refs):
    parts = refs[:2 * n_parts]
    w_ref, x_ref = refs[2 * n_parts], refs[2 * n_parts + 1]
    rest = refs[2 * n_parts + 2:]
    acc = x_ref[...]
    row = 0
    for i in range(n_parts):
        o, g = parts[2 * i][...], parts[2 * i + 1][...]
        mixed = (o * (g / (1.0 + jnp.exp(-g)))).astype(bf16)
        width = mixed.shape[1]
        acc = acc + jnp.dot(mixed, w_ref[row:row + width, :], preferred_element_type=f32)
        row += width
    if final:
        fg_ref, xo_ref, y_ref = rest
        ms = jnp.mean(acc * acc, axis=-1, keepdims=True)
        y_ref[...] = acc * lax.rsqrt(ms + NORM_EPS) * fg_ref[...]
    else:
        (xo_ref,) = rest
    xo_ref[...] = acc


def _out_proj(parts, w16, x2d, final_gain, tm):
    n, d = x2d.shape
    n_parts = len(parts)
    final = final_gain is not None
    in_specs, args = [], []
    for (o, g) in parts:
        for a in (o, g):
            in_specs.append(pl.BlockSpec((tm, a.shape[1]), lambda i: (i, 0)))
            args.append(a)
    in_specs += [pl.BlockSpec(w16.shape, lambda i: (0, 0)), pl.BlockSpec((tm, d), lambda i: (i, 0))]
    args += [w16, x2d]
    out_shape = [jax.ShapeDtypeStruct((n, d), f32)]
    out_specs = [pl.BlockSpec((tm, d), lambda i: (i, 0))]
    if final:
        in_specs.append(pl.BlockSpec((1, d), lambda i: (0, 0)))
        args.append(final_gain)
        out_shape.append(jax.ShapeDtypeStruct((n, d), f32))
        out_specs.append(pl.BlockSpec((tm, d), lambda i: (i, 0)))
    res = pl.pallas_call(
        functools.partial(_out_kernel, n_parts, final),
        grid=(n // tm,),
        in_specs=in_specs,
        out_specs=out_specs,
        out_shape=out_shape,
        compiler_params=_cparams(("parallel",)),
        name="gated_out_proj",
    )(*args)
    return res if final else (res[0], None)


KEY_MAJOR_TILE = 512
FOX_HEADS_PER_STEP = 4
DIFF_HEADS_PER_STEP = 2


def _row_tile(n):
    return 256 if n % 256 == 0 else n


def _key_major(lq, lk, past_len):
    return lq % KEY_MAJOR_TILE == 0 and lk % KEY_MAJOR_TILE == 0 and past_len % KEY_MAJOR_TILE == 0


def _padded_keys(lq, lk, past_len):
    return lk if _key_major(lq, lk, past_len) else -(-lk // LANES) * LANES


def _pad_new_rows(x):
    return jnp.pad(x, ((0, 0), (0, NEW_PAD - x.shape[1]), (0, 0)))


def _even_layer(x, gain, w_in16, w_out16, b_forget, past_len, cache, final_gain):
    b, lq, d = x.shape
    n = b * lq
    lk = past_len + lq
    lkp = _padded_keys(lq, lk, past_len)
    tm = _row_tile(n)
    pos = past_len + jnp.arange(lq)
    cos, sa, sb = (jnp.tile(t, (b, 1)) for t in _rope_tables(pos))
    bias = jnp.zeros((1, LANES), f32).at[0, HEAD_DIM:HEAD_DIM + N_HEADS].set(b_forget.astype(f32))
    (aq16, ak, ak16, av, av16, ag, bq16, bk, bk16, bv, bv16, bg, qi16, misc) = _proj(
        x.reshape(n, d), gain.reshape(1, d), w_in16, cos, sa, sb, bias, EVEN_PLAN, tm)
    ki = misc[:, :HEAD_DIM]
    logf = misc[:, HEAD_DIM:HEAD_DIM + N_HEADS]
    wi = misc[:, HEAD_DIM + N_HEADS:HEAD_DIM + 2 * N_HEADS]

    def r3(a):
        return a.reshape(b, lq, a.shape[-1])

    ki16 = r3(ki).astype(bf16)
    ki2 = jnp.concatenate([ki16, ki16], axis=-1)
    logf3 = r3(logf)
    if cache is None:
        logf_all = logf3
    else:
        logf_all = jnp.pad(jnp.concatenate([cache[2].astype(f32), logf3], axis=1), ((0, 0), (0, lkp - lk), (0, 0)))
    cum = _cumsum_lanes(jnp.swapaxes(logf_all, 1, 2).reshape(b * N_HEADS, lkp)).reshape(b, N_HEADS, lkp)
    cum = cum * LOG2E

    if cache is None:
        assert _key_major(lq, lk, past_len)
        q_aug_t, k_aug, v_aug_t = _fox_operands_t(r3(aq16), r3(ak16), r3(av16), cum, past_len)
        a_out = _fox_attention_t(q_aug_t, k_aug, v_aug_t, past_len, KEY_MAJOR_TILE, KEY_MAJOR_TILE,
                                 FOX_HEADS_PER_STEP)
        b_out = _dsa_attention_t(jnp.swapaxes(r3(bq16), 1, 2), r3(bk16), _values_aug_t(r3(bv16), HEAD_DIM),
                                 jnp.swapaxes(r3(qi16), 1, 2), ki2, jnp.swapaxes(r3(wi), 1, 2),
                                 past_len, lk, KEY_MAJOR_TILE, KEY_MAJOR_TILE)
    else:
        c_ak, c_av, _, c_bk, c_bv, c_bki = cache
        past = c_ak.shape[1]
        assert past == past_len and lq <= NEW_PAD

        def rows(c):
            return c.reshape(b, past, -1)

        pairs = N_HEADS // 2
        cum_new = cum[:, :, past:past + lq]
        cq4 = jnp.swapaxes(cum_new.reshape(b, pairs, 2, lq), 2, 3)
        ckc4 = cum[:, :, :past].reshape(b, pairs, 2, past)
        ckn4 = jnp.pad(cum_new, ((0, 0), (0, 0), (0, NEW_PAD - lq))).reshape(b, pairs, 2, NEW_PAD)
        a_out = _fox_attention_cached(r3(aq16), rows(c_ak), rows(c_av), _pad_new_rows(r3(ak16)),
                                      _pad_new_rows(r3(av16)), cq4, ckc4, ckn4)
        kic = c_bki.astype(bf16)
        b_out = _dsa_attention_cached(r3(bq16), rows(c_bk), rows(c_bv), _pad_new_rows(r3(bk16)),
                                      _pad_new_rows(r3(bv16)), r3(qi16), jnp.concatenate([kic, kic], axis=-1),
                                      _pad_new_rows(ki2), r3(wi), past_len)
    x_new, y = _out_proj([(a_out.reshape(n, -1), ag), (b_out.reshape(n, -1), bg)], w_out16, x.reshape(n, d),
                         final_gain, tm)
    state = (ak.reshape(b, lq, N_HEADS, HEAD_DIM), av.reshape(b, lq, N_HEADS, HEAD_DIM), logf3,
             bk.reshape(b, lq, N_HEADS, HEAD_DIM), bv.reshape(b, lq, N_HEADS, HEAD_DIM), r3(ki))
    return x_new.reshape(b, lq, d), (None if y is None else y.reshape(b, lq, d)), state


def _odd_layer(x, gain, w_in16, lam_params, head_gain, w_out16, lam_init, past_len, cache, final_gain):
    b, lq, d = x.shape
    n = b * lq
    lk = past_len + lq
    tm = _row_tile(n)
    pos = past_len + jnp.arange(lq)
    cos, sa, sb = (jnp.tile(t, (b, 1)) for t in _rope_tables(pos))
    bias = jnp.zeros((1, LANES), f32)
    q16, k, k16, v, v16, g = _proj(x.reshape(n, d), gain.reshape(1, d), w_in16, cos, sa, sb, bias, ODD_PLAN, tm)

    def r3(a):
        return a.reshape(b, lq, a.shape[-1])

    lam_p, hg = lam_params.astype(f32), head_gain.reshape(1, LANES).astype(f32)
    if cache is None:
        assert _key_major(lq, lk, past_len)
        o = _diff_attention_t(jnp.swapaxes(r3(q16), 1, 2), r3(k16), _values_aug_t(r3(v16), LANES), lam_p, hg,
                              lam_init, past_len, lk, KEY_MAJOR_TILE, KEY_MAJOR_TILE, DIFF_HEADS_PER_STEP)
    else:
        c_ck, c_cv = cache
        past = c_ck.shape[1]
        assert past == past_len and lq <= NEW_PAD
        o = _diff_attention_cached(r3(q16), c_ck.reshape(b, past, -1), c_cv.reshape(b, past, -1),
                                   _pad_new_rows(r3(k16)), _pad_new_rows(r3(v16)), lam_p, hg, lam_init, past_len)
    x_new, y = _out_proj([(o.reshape(n, -1), g)], w_out16, x.reshape(n, d), final_gain, tm)
    state = (k.reshape(b, lq, N_HEADS, 2 * HEAD_DIM), v.reshape(b, lq, N_HEADS, 2 * HEAD_DIM))
    return x_new.reshape(b, lq, d), (None if y is None else y.reshape(b, lq, d)), state


def kernel(x_prompt, x_sample, cache_a_k, cache_a_v, cache_a_logf, cache_b_k, cache_b_v, cache_b_kidx, cache_c_k, cache_c_v, norm_gain, final_gain, w_in_even, b_forget, w_out_even, w_in_odd, lambda_params, c_head_gain, w_out_odd):
    past_len = cache_a_k.shape[2]
    depth = norm_gain.shape[0]
    xp, xs = x_prompt, x_sample
    yp = ys = None
    even_p, even_s, odd_p, odd_s = [], [], [], []
    fg = final_gain.reshape(1, -1).astype(f32)
    for layer in range(depth):
        g = norm_gain[layer]
        i = layer // 2
        last = fg if layer == depth - 1 else None
        if layer % 2 == 0:
            w_in16 = _reorder_even_weight(w_in_even[i])
            w_out16 = w_out_even[i].astype(bf16)
            xp, yp, st_p = _even_layer(xp, g, w_in16, w_out16, b_forget[i], 0, None, last)
            xs, ys, st_s = _even_layer(xs, g, w_in16, w_out16, b_forget[i], past_len,
                                       (cache_a_k[i], cache_a_v[i], cache_a_logf[i],
                                        cache_b_k[i], cache_b_v[i], cache_b_kidx[i]), last)
            even_p.append(st_p)
            even_s.append(st_s)
        else:
            lam_init = 0.8 - 0.6 * math.exp(-0.3 * layer)
            w_in16 = w_in_odd[i].astype(bf16)
            w_out16 = w_out_odd[i].astype(bf16)
            xp, yp, st_p = _odd_layer(xp, g, w_in16, lambda_params[i], c_head_gain[i], w_out16, lam_init,
                                      0, None, last)
            xs, ys, st_s = _odd_layer(xs, g, w_in16, lambda_params[i], c_head_gain[i], w_out16, lam_init,
                                      past_len, (cache_c_k[i], cache_c_v[i]), last)
            odd_p.append(st_p)
            odd_s.append(st_s)
    a_k_p, a_v_p, a_f_p, b_k_p, b_v_p, b_i_p = [jnp.stack(t) for t in zip(*even_p)]
    a_k_s, a_v_s, a_f_s, b_k_s, b_v_s, b_i_s = [jnp.stack(t) for t in zip(*even_s)]
    c_k_p, c_v_p = [jnp.stack(t) for t in zip(*odd_p)]
    c_k_s, c_v_s = [jnp.stack(t) for t in zip(*odd_s)]
    return (yp, ys,
            a_k_p, a_v_p, a_f_p, b_k_p, b_v_p, b_i_p, c_k_p, c_v_p,
            a_k_s, a_v_s, a_f_s, b_k_s, b_v_s, b_i_s, c_k_s, c_v_s)
```

```python
import functools
import math

import jax
import jax.numpy as jnp
from jax import lax
from jax.experimental import pallas as pl
from jax.experimental.pallas import tpu as pltpu

LANES = 128
HEAD_DIM = 64
CHUNK = 64
CHUNK_SHIFT = 6
ROT_DIM = HEAD_DIM // 4
ROPE_THETA = 500000.0
IDX_TOPK_MAX = 256
N_HEADS = 8
NORM_EPS = 1e-6
NEG_INF = -1e30
QK_SCALE = HEAD_DIM ** -0.5
LOG2E = 1.4426950408889634
Q_SCALE = QK_SCALE * LOG2E
VALUE_PAD = 16
NEW_PAD = LANES
INT_MIN = -2 ** 31
VMEM_LIMIT = 56 * 1024 * 1024
REDUCE_CHAINS = 8
FOX_BIAS_COLS = 6

f32 = jnp.float32
bf16 = jnp.bfloat16
i32 = jnp.int32


def _cparams(sem):
    return pltpu.CompilerParams(dimension_semantics=sem, vmem_limit_bytes=VMEM_LIMIT)


def _rope_group(r, cos, sa, sb):
    return r * cos + pltpu.roll(r, 8, 1) * sa + pltpu.roll(r, LANES - 8, 1) * sb


def _proj_kernel(plan, x_ref, g_ref, w_ref, cos_ref, sa_ref, sb_ref, bias_ref, *out_refs):
    x = x_ref[...]
    ms = jnp.mean(x * x, axis=-1, keepdims=True)
    h = (x * lax.rsqrt(ms + NORM_EPS) * g_ref[...]).astype(bf16)
    cos, sa, sb = cos_ref[...], sa_ref[...], sb_ref[...]
    lane = lax.broadcasted_iota(i32, (1, LANES), 1)
    oi = 0
    for (c0, width, kind, outs) in plan:
        r = jnp.dot(h, w_ref[:, c0:c0 + width], preferred_element_type=f32)
        if kind == "rope":
            r = jnp.concatenate(
                [_rope_group(r[:, g * LANES:(g + 1) * LANES], cos, sa, sb) for g in range(width // LANES)],
                axis=1)
        elif kind == "misc":
            low = lane < HEAD_DIM
            roped = _rope_group(r, jnp.where(low, cos, 1.0), jnp.where(low, sa, 0.0), jnp.where(low, sb, 0.0))
            z = r + bias_ref[...]
            logf = jnp.minimum(z, 0.0) - jnp.log(1.0 + jnp.exp(-jnp.abs(z)))
            r = jnp.where((lane >= HEAD_DIM) & (lane < HEAD_DIM + N_HEADS), logf, roped)
        for (dtype, scale) in outs:
            val = r if scale == 1.0 else r * scale
            out_refs[oi][...] = val.astype(dtype)
            oi += 1


def _proj(x2d, gain, w16, cos, sa, sb, bias, plan, tm):
    n, d = x2d.shape
    wtot = w16.shape[1]
    out_shapes, out_specs = [], []
    for (c0, width, kind, outs) in plan:
        for (dtype, _) in outs:
            out_shapes.append(jax.ShapeDtypeStruct((n, width), dtype))
            out_specs.append(pl.BlockSpec((tm, width), lambda i: (i, 0)))
    return pl.pallas_call(
        functools.partial(_proj_kernel, plan),
        grid=(n // tm,),
        in_specs=[
            pl.BlockSpec((tm, d), lambda i: (i, 0)),
            pl.BlockSpec((1, d), lambda i: (0, 0)),
            pl.BlockSpec((d, wtot), lambda i: (0, 0)),
            pl.BlockSpec((tm, LANES), lambda i: (i, 0)),
            pl.BlockSpec((tm, LANES), lambda i: (i, 0)),
            pl.BlockSpec((tm, LANES), lambda i: (i, 0)),
            pl.BlockSpec((1, LANES), lambda i: (0, 0)),
        ],
        out_specs=out_specs,
        out_shape=out_shapes,
        compiler_params=_cparams(("parallel",)),
        name="norm_in_proj",
    )(x2d, gain, w16, cos, sa, sb, bias)


EVEN_PLAN = (
    (0, 512, "plain", ((bf16, Q_SCALE),)),
    (512, 512, "plain", ((f32, 1.0), (bf16, 1.0))),
    (1024, 512, "plain", ((f32, 1.0), (bf16, 1.0))),
    (1536, 512, "plain", ((f32, 1.0),)),
    (2048, 512, "rope", ((bf16, Q_SCALE),)),
    (2560, 512, "rope", ((f32, 1.0), (bf16, 1.0))),
    (3072, 512, "plain", ((f32, 1.0), (bf16, 1.0))),
    (3584, 512, "plain", ((f32, 1.0),)),
    (4096, 512, "rope", ((bf16, QK_SCALE),)),
    (4608, LANES, "misc", ((f32, 1.0),)),
)
ODD_PLAN = (
    (0, 1024, "rope", ((bf16, Q_SCALE),)),
    (1024, 1024, "rope", ((f32, 1.0), (bf16, 1.0))),
    (2048, 1024, "plain", ((f32, 1.0), (bf16, 1.0))),
    (3072, 1024, "plain", ((f32, 1.0),)),
)


def _reorder_even_weight(w):
    a4 = w[:, 0:2048]
    af = w[:, 2048:2056]
    b4 = w[:, 2056:4104]
    qi = w[:, 4104:4616]
    ki = w[:, 4616:4680]
    wi = w[:, 4680:4688]
    pad = jnp.zeros((w.shape[0], LANES - HEAD_DIM - 2 * N_HEADS), w.dtype)
    return jnp.concatenate([a4, b4, qi, ki, af, wi, pad], axis=1).astype(bf16)


def _rope_tables(pos):
    half = ROT_DIM // 2
    inv = ROPE_THETA ** (-jnp.arange(half, dtype=f32) * 2.0 / ROT_DIM)
    ang = pos.astype(f32)[:, None] * inv[None, :]
    cos, sin = jnp.cos(ang), jnp.sin(ang)
    n = pos.shape[0]
    rest = HEAD_DIM - ROT_DIM
    cos64 = jnp.concatenate([cos, cos, jnp.ones((n, rest), f32)], axis=1)
    sa64 = jnp.concatenate([jnp.zeros((n, half), f32), sin, jnp.zeros((n, rest), f32)], axis=1)
    sb64 = jnp.concatenate([-sin, jnp.zeros((n, half + rest), f32)], axis=1)
    return tuple(jnp.concatenate([t, t], axis=1) for t in (cos64, sa64, sb64))


def _cumsum_kernel(x_ref, o_ref):
    rows, length = x_ref.shape
    lane = lax.broadcasted_iota(i32, (rows, LANES), 1)

    def body(j, carry):
        off = pl.multiple_of(j * LANES, LANES)
        x = x_ref[:, pl.ds(off, LANES)]
        for s in (1, 2, 4, 8, 16, 32, 64):
            x = x + jnp.where(lane >= s, pltpu.roll(x, s, 1), 0.0)
        x = x + carry
        o_ref[:, pl.ds(off, LANES)] = x
        return jnp.broadcast_to(x[:, LANES - 1:LANES], (rows, LANES))

    lax.fori_loop(0, length // LANES, body, jnp.zeros((rows, LANES), f32))


def _cumsum_lanes(x):
    rows, length = x.shape
    return pl.pallas_call(
        _cumsum_kernel,
        grid=(1,),
        in_specs=[pl.BlockSpec((rows, length), lambda i: (0, 0))],
        out_specs=pl.BlockSpec((rows, length), lambda i: (0, 0)),
        out_shape=jax.ShapeDtypeStruct((rows, length), f32),
        compiler_params=_cparams(("parallel",)),
        name="logf_cumsum",
    )(x)


def _dot_nt(a, b):
    return lax.dot_general(a, b, (((1,), (1,)), ((), ())), preferred_element_type=f32)


def _lane_halves(q):
    lane = lax.broadcasted_iota(i32, q.shape, 1)
    zero = jnp.zeros_like(q)
    return jnp.where(lane < HEAD_DIM, q, zero), jnp.where(lane >= HEAD_DIM, q, zero)


def _two_source_attention(s_c, s_n, vc, vn):
    m = jnp.maximum(jnp.max(s_c, axis=1, keepdims=True), jnp.max(s_n, axis=1, keepdims=True))
    p_c, p_n = jnp.exp2(s_c - m), jnp.exp2(s_n - m)
    l = jnp.sum(p_c, axis=1, keepdims=True) + jnp.sum(p_n, axis=1, keepdims=True)
    o = (jnp.dot(p_c.astype(bf16), vc, preferred_element_type=f32)
         + jnp.dot(p_n.astype(bf16), vn, preferred_element_type=f32))
    return o / l


def _order_key(x):
    b = lax.bitcast_convert_type(x, i32)
    return b ^ ((b >> 31) & jnp.int32(0x7FFFFFFF))


def _new_row_positions(lq):
    return lax.broadcasted_iota(i32, (lq, 1), 0), lax.broadcasted_iota(i32, (1, NEW_PAD), 1)


def _cache_spec(cache4, layer, width=None):
    _, _, past, w = cache4.shape
    return pl.BlockSpec((1, 1, past, w if width is None else width), lambda bi: (layer, bi, 0, 0))


def _stream_spec(rows, width):
    return pl.BlockSpec((1, rows, width), lambda bi: (bi, 0, 0))


def _pair_lanes(g):
    return slice(g * LANES, (g + 1) * LANES)


def _fox_cached_kernel(q_ref, kc_ref, vc_ref, kn_ref, vn_ref, cq_ref, ckc_ref, ckn_ref, o_ref, *, lq):
    cq = cq_ref[0]
    q_idx, k_idx = _new_row_positions(lq)
    visible = k_idx <= q_idx
    lane = lax.broadcasted_iota(i32, (lq, LANES), 1)
    for g in range(N_HEADS // 2):
        cols = _pair_lanes(g)
        qh = _lane_halves(q_ref[0, :, cols])
        kc, vc = kc_ref[0, 0, :, cols].astype(bf16), vc_ref[0, 0, :, cols].astype(bf16)
        kn, vn = kn_ref[0, :, cols], vn_ref[0, :, cols]
        outs = []
        for hh in range(2):
            h = 2 * g + hh
            s_c = _dot_nt(qh[hh], kc) + cq[:, h:h + 1] - ckc_ref[0, h:h + 1, :]
            s_n = _dot_nt(qh[hh], kn) + cq[:, h:h + 1] - ckn_ref[0, h:h + 1, :]
            outs.append(_two_source_attention(s_c, jnp.where(visible, s_n, NEG_INF), vc, vn))
        o_ref[0, :, cols] = jnp.where(lane < HEAD_DIM, outs[0], outs[1])


def _fox_attention_cached(q16, kc4, vc4, layer, kn16, vn16, cq, ckc, ckn):
    b, lq, width = q16.shape
    past = kc4.shape[2]
    return pl.pallas_call(
        functools.partial(_fox_cached_kernel, lq=lq),
        grid=(b,),
        in_specs=[
            _stream_spec(lq, width), _cache_spec(kc4, layer), _cache_spec(vc4, layer),
            _stream_spec(NEW_PAD, width), _stream_spec(NEW_PAD, width),
            _stream_spec(lq, N_HEADS), _stream_spec(N_HEADS, past), _stream_spec(N_HEADS, NEW_PAD),
        ],
        out_specs=_stream_spec(lq, width),
        out_shape=jax.ShapeDtypeStruct((b, lq, width), f32),
        compiler_params=_cparams(("parallel",)),
        name="fox_attention_cached",
    )(q16, kc4, vc4, kn16, vn16, cq, ckc, ckn)


def _diff_lambda(lp_ref, lam_init):
    lp = lp_ref[...]
    return (jnp.exp(jnp.sum(lp[0:1] * lp[1:2], axis=1, keepdims=True))
            - jnp.exp(jnp.sum(lp[2:3] * lp[3:4], axis=1, keepdims=True)) + lam_init)


def _head_norm(o, hg_ref, lam_init):
    ms = jnp.mean(o * o, axis=-1, keepdims=True)
    return (o * lax.rsqrt(ms + NORM_EPS) * hg_ref[...]) * (1.0 - lam_init)


def _new_chunk_valid(lq, past_len):
    q_idx, k_idx = _new_row_positions(lq)
    return (((past_len + k_idx) >> CHUNK_SHIFT) <= ((past_len + q_idx) >> CHUNK_SHIFT)) & (k_idx < lq)


def _diff_cached_kernel(q_ref, kc_ref, vc_ref, kn_ref, vn_ref, lp_ref, hg_ref, o_ref, *, lq, past_len, lam_init):
    valid = _new_chunk_valid(lq, past_len)
    lam = _diff_lambda(lp_ref, lam_init)
    for h in range(N_HEADS):
        cols = _pair_lanes(h)
        qh = _lane_halves(q_ref[0, :, cols])
        kc, vc = kc_ref[0, 0, :, cols].astype(bf16), vc_ref[0, 0, :, cols].astype(bf16)
        kn, vn = kn_ref[0, :, cols], vn_ref[0, :, cols]
        sub = [_two_source_attention(_dot_nt(qh[c], kc), jnp.where(valid, _dot_nt(qh[c], kn), NEG_INF), vc, vn)
               for c in range(2)]
        o_ref[0, :, cols] = _head_norm(sub[0] - lam * sub[1], hg_ref, lam_init)


def _diff_attention_cached(q16, kc4, vc4, layer, kn16, vn16, lam_params, head_gain, lam_init, past_len):
    b, lq, width = q16.shape
    return pl.pallas_call(
        functools.partial(_diff_cached_kernel, lq=lq, past_len=past_len, lam_init=lam_init),
        grid=(b,),
        in_specs=[
            _stream_spec(lq, width), _cache_spec(kc4, layer), _cache_spec(vc4, layer),
            _stream_spec(NEW_PAD, width), _stream_spec(NEW_PAD, width),
            pl.BlockSpec((4, HEAD_DIM), lambda bi: (0, 0)),
            pl.BlockSpec((1, LANES), lambda bi: (0, 0)),
        ],
        out_specs=_stream_spec(lq, width),
        out_shape=jax.ShapeDtypeStruct((b, lq, width), f32),
        compiler_params=_cparams(("parallel",)),
        name="diff_attention_cached",
    )(q16, kc4, vc4, kn16, vn16, lam_params, head_gain)


def _dsa_cached_kernel(q_ref, kc_ref, vc_ref, kn_ref, vn_ref, qi_ref, kic_ref, kin_ref, wi_ref, o_ref,
                       key_sc, *, lq, past, past_len, topk, idx_bits):
    n_keys = past + NEW_PAD
    n_groups = n_keys // LANES
    valid_new = _new_chunk_valid(lq, past_len)

    qi_h = []
    for g in range(N_HEADS // 2):
        qi_h.extend(_lane_halves(qi_ref[0, :, _pair_lanes(g)]))
    wi = wi_ref[0] * (N_HEADS ** -0.5)

    def scores(ki_blk):
        score = jnp.zeros((lq, ki_blk.shape[0]), f32)
        for h in range(N_HEADS):
            score = score + jnp.maximum(_dot_nt(qi_h[h], ki_blk), 0.0) * wi[:, h:h + 1]
        return score

    key_sc[:, 0:past] = _order_key(scores(kic_ref[0]))
    key_sc[:, past:n_keys] = _order_key(jnp.where(valid_new, scores(kin_ref[0]), NEG_INF))

    def count(indicator):
        acc = indicator[:, 0:LANES]
        for g in range(1, n_groups):
            acc = acc + indicator[:, g * LANES:(g + 1) * LANES]
        return jnp.sum(acc, axis=1, keepdims=True)

    def bit_body(i, thr):
        cand = thr ^ jnp.left_shift(jnp.int32(1), 31 - i)
        cnt = count(jnp.where(key_sc[...] >= cand, 1.0, 0.0))
        return jnp.where(cnt >= topk, cand, thr)

    thr = lax.fori_loop(0, 32, bit_body, jnp.full((lq, 1), INT_MIN, i32))

    key = key_sc[...]
    k_pos = lax.broadcasted_iota(i32, (1, n_keys), 1)
    need = topk - count(jnp.where(key > thr, 1.0, 0.0))
    excess = jnp.max(jnp.where(count(jnp.where(key == thr, 1.0, 0.0)) > need, 1.0, 0.0)) > 0.0

    def tie_search():
        def tbit(i, last):
            cand = last | jnp.left_shift(jnp.int32(1), idx_bits - 1 - i)
            cnt = count(jnp.where(key_sc[...] == thr, jnp.where(k_pos < cand, 1.0, 0.0), 0.0))
            return jnp.where(cnt < need, cand, last)
        return lax.fori_loop(0, idx_bits, tbit, jnp.zeros((lq, 1), i32))

    tie_last = lax.cond(excess, tie_search, lambda: jnp.full((lq, 1), 2 ** idx_bits, i32))

    tie_bias = jnp.where(k_pos <= tie_last, 0.0, NEG_INF)
    bias = jnp.where(key > thr, 0.0, jnp.where(key == thr, tie_bias, NEG_INF))
    bias_c = bias[:, 0:past]
    bias_n = jnp.where(valid_new, bias[:, past:n_keys], NEG_INF)
    lane = lax.broadcasted_iota(i32, (lq, LANES), 1)
    for g in range(N_HEADS // 2):
        cols = _pair_lanes(g)
        q_pair = _lane_halves(q_ref[0, :, cols])
        kc, vc = kc_ref[0, 0, :, cols].astype(bf16), vc_ref[0, 0, :, cols].astype(bf16)
        kn, vn = kn_ref[0, :, cols], vn_ref[0, :, cols]
        outs = [_two_source_attention(_dot_nt(q_pair[hh], kc) + bias_c, _dot_nt(q_pair[hh], kn) + bias_n, vc, vn)
                for hh in range(2)]
        o_ref[0, :, cols] = jnp.where(lane < HEAD_DIM, outs[0], outs[1])


def _dsa_attention_cached(q16, kc4, vc4, layer, kn16, vn16, qi16, kic2, kin2, wi, past_len):
    b, lq, width = q16.shape
    past = kc4.shape[2]
    topk = min(IDX_TOPK_MAX, (past + lq) // 4)
    idx_bits = max(1, int(past + NEW_PAD).bit_length())
    assert past >= topk and past % LANES == 0
    return pl.pallas_call(
        functools.partial(_dsa_cached_kernel, lq=lq, past=past, past_len=past_len, topk=topk, idx_bits=idx_bits),
        grid=(b,),
        in_specs=[
            _stream_spec(lq, width), _cache_spec(kc4, layer), _cache_spec(vc4, layer),
            _stream_spec(NEW_PAD, width), _stream_spec(NEW_PAD, width),
            _stream_spec(lq, width), _stream_spec(past, LANES), _stream_spec(NEW_PAD, LANES),
            _stream_spec(lq, N_HEADS),
        ],
        out_specs=_stream_spec(lq, width),
        out_shape=jax.ShapeDtypeStruct((b, lq, width), f32),
        scratch_shapes=[pltpu.VMEM((lq, past + NEW_PAD), i32)],
        compiler_params=_cparams(("parallel",)),
        name="dsa_attention_cached",
    )(q16, kc4, vc4, kn16, vn16, qi16, kic2, kin2, wi)


def _resident_spec(block_shape, index_map):
    return pl.BlockSpec(block_shape, index_map, pipeline_mode=pl.Buffered(1))


def _tree_fold(x, op):
    slab = x.shape[0] // REDUCE_CHAINS
    parts = [x[i * slab:(i + 1) * slab] for i in range(REDUCE_CHAINS)]
    while len(parts) > 1:
        parts = [op(parts[i], parts[i + 1]) for i in range(0, len(parts), 2)]
    return parts[0]


def _softmax_step_t(s, v_aug, hidx, m_sc, acc_sc):
    m_old = m_sc[hidx]
    m_new = jnp.maximum(m_old, jnp.max(_tree_fold(s, jnp.maximum), axis=0, keepdims=True))
    alpha = jnp.exp2(m_old - m_new)
    p = jnp.exp2(s - m_new)
    acc_sc[hidx] = alpha * acc_sc[hidx] + jnp.dot(v_aug, p.astype(bf16), preferred_element_type=f32)
    m_sc[hidx] = m_new


def _init_softmax_t(m_sc, acc_sc):
    m_sc[...] = jnp.full(m_sc.shape, NEG_INF, f32)
    acc_sc[...] = jnp.zeros(acc_sc.shape, f32)


def _normalized_t(acc_sc, hidx, dv):
    acc = acc_sc[hidx]
    return acc[:dv] / acc[dv:dv + 1]


def _values_aug_t(v16, dv):
    b, lk, width = v16.shape
    nh = width // dv
    v_t = jnp.swapaxes(v16, 1, 2).reshape(b, nh, dv, lk)
    ones = jnp.ones((b, nh, 1, lk), bf16)
    zeros = jnp.zeros((b, nh, VALUE_PAD - 1, lk), bf16)
    return jnp.concatenate([v_t, ones, zeros], axis=2).reshape(b, nh * (dv + VALUE_PAD), lk)


def _score_tile(buf, k_ref, lane0, k0, tk, q_t):
    half = tk // 2
    for i in range(2):
        buf[i * half:(i + 1) * half, :] = jnp.dot(
            k_ref[0, pl.ds(k0 + i * half, half), lane0:lane0 + LANES], q_t, preferred_element_type=f32)


def _score_pipeline(n_chains, n_kb, qk_fn, sm_fn, bufs):
    assert n_chains % 2 == 0
    qk_fn(0, 0, bufs[0])

    def body(kb, carry):
        for c in range(n_chains):
            if c + 1 < n_chains:
                qk_fn(kb, c + 1, bufs[(c + 1) % 2])
            else:
                qk_fn(kb + 1, 0, bufs[0])
            sm_fn(kb, c, bufs[c % 2], False)
        return carry

    lax.fori_loop(0, n_kb - 1, body, 0)
    for c in range(n_chains):
        if c + 1 < n_chains:
            qk_fn(n_kb - 1, c + 1, bufs[(c + 1) % 2])
        sm_fn(n_kb - 1, c, bufs[c % 2], True)


def _half_rows(x, upper):
    row = lax.broadcasted_iota(i32, x.shape, 0)
    keep = (row >= HEAD_DIM) if upper else (row < HEAD_DIM)
    return jnp.where(keep, x, jnp.zeros_like(x))


def _chunk_valid_t(first_q, k0, tq, tk, lk):
    k_pos = k0 + lax.broadcasted_iota(i32, (tk, tq), 0)
    q_pos = first_q + lax.broadcasted_iota(i32, (tk, tq), 1)
    return ((k_pos >> CHUNK_SHIFT) <= (q_pos >> CHUNK_SHIFT)) & (k_pos < lk)


def _store_head_pairs_t(o_ref, heads_t):
    for g in range(len(heads_t) // 2):
        o_t = jnp.concatenate([heads_t[2 * g], heads_t[2 * g + 1]], axis=0)
        o_ref[0, :, g * LANES:(g + 1) * LANES] = o_t.T


def _fox_t_kernel(q_ref, k_ref, v_ref, o_ref, m_sc, acc_sc, sa_sc, sb_sc, *, tq, tk, past_len, nh):
    qb = pl.program_id(2)
    first_q = past_len + qb * tq
    n_kb = first_q // tk + 1
    dva = HEAD_DIM + VALUE_PAD
    _init_softmax_t(m_sc, acc_sc)

    def qk(kb, c, buf):
        k0 = pl.multiple_of(kb * tk, tk)
        _score_tile(buf, k_ref, c * LANES, k0, tk, q_ref[0, c * LANES:(c + 1) * LANES, :])

    def sm(kb, c, buf, masked):
        k0 = pl.multiple_of(kb * tk, tk)
        s = buf[...]
        if masked:
            k_pos = k0 + lax.broadcasted_iota(i32, (tk, tq), 0)
            q_pos = first_q + lax.broadcasted_iota(i32, (tk, tq), 1)
            s = jnp.where(k_pos <= q_pos, s, NEG_INF)
        _softmax_step_t(s, v_ref[0, c * dva:(c + 1) * dva, pl.ds(k0, tk)], c, m_sc, acc_sc)

    _score_pipeline(nh, n_kb, qk, sm, (sa_sc, sb_sc))
    _store_head_pairs_t(o_ref, [_normalized_t(acc_sc, c, HEAD_DIM) for c in range(nh)])


def _fox_attention_t(q_aug_t, k_aug, v_aug_t, past_len, tq, tk, nh):
    b, _, lq = q_aug_t.shape
    lkp = k_aug.shape[1]
    assert tk % tq == 0 and past_len % tk == 0 and lkp % tk == 0
    return pl.pallas_call(
        functools.partial(_fox_t_kernel, tq=tq, tk=tk, past_len=past_len, nh=nh),
        grid=(b, N_HEADS // nh, lq // tq),
        in_specs=[
            pl.BlockSpec((1, nh * LANES, tq), lambda bi, g, qi: (bi, g, qi)),
            _resident_spec((1, lkp, nh * LANES), lambda bi, g, qi: (bi, 0, g)),
            _resident_spec((1, nh * (HEAD_DIM + VALUE_PAD), lkp), lambda bi, g, qi: (bi, g, 0)),
        ],
        out_specs=pl.BlockSpec((1, tq, nh * HEAD_DIM), lambda bi, g, qi: (bi, qi, g)),
        out_shape=jax.ShapeDtypeStruct((b, lq, N_HEADS * HEAD_DIM), f32),
        scratch_shapes=[
            pltpu.VMEM((nh, 1, tq), f32),
            pltpu.VMEM((nh, HEAD_DIM + VALUE_PAD, tq), f32),
            pltpu.VMEM((tk, tq), f32),
            pltpu.VMEM((tk, tq), f32),
        ],
        compiler_params=_cparams(("parallel", "parallel", "arbitrary")),
        name="fox_attention_t",
    )(q_aug_t, k_aug, v_aug_t)


def _bf16_prefix(x):
    bits = lax.bitcast_convert_type(x, jnp.uint32) & jnp.uint32(0xFFFF0000)
    return lax.bitcast_convert_type(bits, f32)


def _split3(x):
    x1 = _bf16_prefix(x)
    r = x - x1
    x2 = _bf16_prefix(r)
    return x1.astype(bf16), x2.astype(bf16), (r - x2).astype(bf16)


def _fox_operands_t(aq16, ak16, av16, cum, past_len):
    b, lq, _ = aq16.shape
    lk = ak16.shape[1]
    pad = LANES - HEAD_DIM - FOX_BIAS_COLS
    ck = [jnp.swapaxes(c, 1, 2)[..., None] for c in _split3(cum)]
    one_k = jnp.ones((b, lk, N_HEADS, 1), bf16)
    k_aug = jnp.concatenate(
        [ak16.reshape(b, lk, N_HEADS, HEAD_DIM), one_k, one_k, one_k, -ck[0], -ck[1], -ck[2],
         jnp.zeros((b, lk, N_HEADS, pad), bf16)], axis=-1).reshape(b, lk, N_HEADS * LANES)
    cq = [c[:, :, None, :] for c in _split3(cum[:, :, past_len:past_len + lq])]
    one_q = jnp.ones((b, N_HEADS, 1, lq), bf16)
    q_t = jnp.transpose(aq16.reshape(b, lq, N_HEADS, HEAD_DIM), (0, 2, 3, 1))
    q_aug_t = jnp.concatenate(
        [q_t, cq[0], cq[1], cq[2], one_q, one_q, one_q, jnp.zeros((b, N_HEADS, pad, lq), bf16)],
        axis=2).reshape(b, N_HEADS * LANES, lq)
    return q_aug_t, k_aug, _values_aug_t(av16, HEAD_DIM)


def _diff_t_kernel(q_ref, k_ref, v_ref, lp_ref, hg_ref, o_ref, m_sc, acc_sc, sa_sc, sb_sc, *,
                   tq, tk, past_len, lk, lam_init, nh):
    qb = pl.program_id(2)
    first_q = past_len + qb * tq
    n_kb = first_q // tk + 1
    dva = LANES + VALUE_PAD
    _init_softmax_t(m_sc, acc_sc)

    def qk(kb, c, buf):
        k0 = pl.multiple_of(kb * tk, tk)
        h = c // 2
        q_c = _half_rows(q_ref[0, h * LANES:(h + 1) * LANES, :], c % 2 == 1)
        _score_tile(buf, k_ref, h * LANES, k0, tk, q_c)

    def sm(kb, c, buf, masked):
        k0 = pl.multiple_of(kb * tk, tk)
        h = c // 2
        s = buf[...]
        if masked:
            s = jnp.where(_chunk_valid_t(first_q, k0, tq, tk, lk), s, NEG_INF)
        _softmax_step_t(s, v_ref[0, h * dva:(h + 1) * dva, pl.ds(k0, tk)], c, m_sc, acc_sc)

    _score_pipeline(2 * nh, n_kb, qk, sm, (sa_sc, sb_sc))
    lam = _diff_lambda(lp_ref, lam_init)
    for h in range(nh):
        o_t = _normalized_t(acc_sc, 2 * h, LANES) - lam * _normalized_t(acc_sc, 2 * h + 1, LANES)
        o_ref[0, :, h * LANES:(h + 1) * LANES] = _head_norm(o_t.T, hg_ref, lam_init)


def _diff_attention_t(q_t, k16, v_t, lam_params, head_gain, lam_init, past_len, lk, tq, tk, nh):
    b, width, lq = q_t.shape
    lkp = k16.shape[1]
    n_heads = width // LANES
    assert tk % tq == 0 and past_len % tk == 0 and lkp % tk == 0 and tq % CHUNK == 0
    return pl.pallas_call(
        functools.partial(_diff_t_kernel, tq=tq, tk=tk, past_len=past_len, lk=lk, lam_init=lam_init, nh=nh),
        grid=(b, n_heads // nh, lq // tq),
        in_specs=[
            pl.BlockSpec((1, nh * LANES, tq), lambda bi, g, qi: (bi, g, qi)),
            _resident_spec((1, lkp, nh * LANES), lambda bi, g, qi: (bi, 0, g)),
            _resident_spec((1, nh * (LANES + VALUE_PAD), lkp), lambda bi, g, qi: (bi, g, 0)),
            pl.BlockSpec((4, HEAD_DIM), lambda bi, g, qi: (0, 0)),
            pl.BlockSpec((1, LANES), lambda bi, g, qi: (0, 0)),
        ],
        out_specs=pl.BlockSpec((1, tq, nh * LANES), lambda bi, g, qi: (bi, qi, g)),
        out_shape=jax.ShapeDtypeStruct((b, lq, width), f32),
        scratch_shapes=[
            pltpu.VMEM((2 * nh, 1, tq), f32),
            pltpu.VMEM((2 * nh, LANES + VALUE_PAD, tq), f32),
            pltpu.VMEM((tk, tq), f32),
            pltpu.VMEM((tk, tq), f32),
        ],
        compiler_params=_cparams(("parallel", "parallel", "arbitrary")),
        name="diff_attention_t",
    )(q_t, k16, v_t, lam_params, head_gain)


def _fold_rows(x, rows_out, op):
    n = x.shape[0] // rows_out
    chains = min(FOLD_CHAINS, n)
    accs = [x[j * rows_out:(j + 1) * rows_out] for j in range(chains)]
    for i in range(chains, n):
        accs[i % chains] = op(accs[i % chains], x[i * rows_out:(i + 1) * rows_out])
    while len(accs) > 1:
        accs = [op(accs[i], accs[i + 1]) for i in range(0, len(accs), 2)]
    return accs[0]


FOLD_CHAINS = 4
HALF_BIAS = 32768
PACKED_ROWS = 16


def _dsa_t_kernel(q_ref, k_ref, v_ref, qi_ref, ki_ref, wi_ref, o_ref, hi_sc, lo_sc, bias_sc, tri_sc, tie_room_sc,
                  m_sc, acc_sc, sa_sc, sb_sc, *, tq, tk, past_len, lk, topk):
    qb = pl.program_id(1)
    first_q = past_len + qb * tq
    n_kb = first_q // tk + 1

    def head_rows(ref, h):
        g = h // 2
        return _half_rows(ref[0, g * LANES:(g + 1) * LANES, :], h % 2 == 1)

    wi = wi_ref[0] * (N_HEADS ** -0.5)

    def score_block(kb, masked):
        k0 = pl.multiple_of(kb * tk, tk)
        ki_blk = ki_ref[0, pl.ds(k0, tk), :]
        score = jnp.zeros((tk, tq), f32)
        for h in range(N_HEADS):
            d = jnp.dot(ki_blk, head_rows(qi_ref, h), preferred_element_type=f32)
            score = score + jnp.maximum(d, 0.0) * wi[h:h + 1, :]
        if masked:
            score = jnp.where(_chunk_valid_t(first_q, k0, tq, tk, lk), score, NEG_INF)
        key = _order_key(score)
        hi_sc[pl.ds(k0, tk), :] = (key >> 16).astype(jnp.int16)
        lo_sc[pl.ds(k0, tk), :] = ((key & 0xFFFF) - HALF_BIAS).astype(jnp.int16)

    def score_body(kb, c):
        score_block(kb, False)
        return c

    lax.fori_loop(0, n_kb - 1, score_body, 0)
    score_block(n_kb - 1, True)

    one16, zero16 = jnp.int16(1), jnp.int16(0)

    def count16(indicator_fn):
        def body(kb, c):
            k0 = pl.multiple_of(kb * tk, tk)
            ind = indicator_fn(hi_sc[pl.ds(k0, tk), :], lo_sc[pl.ds(k0, tk), :])
            return c + _fold_rows(ind, PACKED_ROWS, jnp.add)
        c = lax.fori_loop(0, n_kb, body, jnp.zeros((PACKED_ROWS, tq), jnp.int16))
        return jnp.sum(c.astype(i32), axis=0, keepdims=True)

    def search16(use_lo, need):
        def bit_body(i, t_off):
            cand = t_off | jnp.left_shift(jnp.int32(1), 15 - i)
            c16 = (cand - HALF_BIAS).astype(jnp.int16)
            cnt = count16(lambda hi, lo: jnp.where((lo if use_lo else hi) >= c16, one16, zero16))
            return jnp.where(cnt >= need, cand, t_off)
        return lax.fori_loop(0, 16, bit_body, jnp.zeros((1, tq), i32))

    thi = search16(False, topk) - HALF_BIAS
    thi16 = thi.astype(jnp.int16)
    cnt_hi_gt = count16(lambda hi, lo: jnp.where(hi > thi16, one16, zero16))

    def keep_lo_body(kb, c):
        k0 = pl.multiple_of(kb * tk, tk)
        lo_sc[pl.ds(k0, tk), :] = jnp.where(hi_sc[pl.ds(k0, tk), :] == thi16, lo_sc[pl.ds(k0, tk), :],
                                            jnp.int16(-HALF_BIAS))
        return c

    lax.fori_loop(0, n_kb, keep_lo_body, 0)
    tlo16 = (search16(True, topk - cnt_hi_gt) - HALF_BIAS).astype(jnp.int16)
    cnt_gt = cnt_hi_gt + count16(lambda hi, lo: jnp.where(lo > tlo16, one16, zero16))

    dva = HEAD_DIM + VALUE_PAD
    _init_softmax_t(m_sc, acc_sc)
    tie_room_sc[...] = (topk - cnt_gt).astype(f32)
    tri_sc[...] = jnp.where(lax.broadcasted_iota(i32, (tk, tk), 1) <= lax.broadcasted_iota(i32, (tk, tk), 0),
                            1.0, 0.0).astype(bf16)

    def qk(kb, h, buf):
        k0 = pl.multiple_of(kb * tk, tk)
        _score_tile(buf, k_ref, (h // 2) * LANES, k0, tk, head_rows(q_ref, h))

    def sm(kb, h, buf, masked):
        k0 = pl.multiple_of(kb * tk, tk)
        if h == 0:
            hi, lo = hi_sc[pl.ds(k0, tk), :], lo_sc[pl.ds(k0, tk), :]
            two_b, one_b, zero_b = jnp.bfloat16(2.0), jnp.bfloat16(1.0), jnp.bfloat16(0.0)
            low_code = jnp.where(lo > tlo16, two_b, jnp.where(lo == tlo16, one_b, zero_b))
            code_b = jnp.where(hi > thi16, two_b, jnp.where(hi == thi16, low_code, zero_b))
            tied = jnp.where(code_b == one_b, one_b, zero_b)
            rank = jnp.dot(tri_sc[...], tied, preferred_element_type=f32)
            room = tie_room_sc[...]
            x = code_b.astype(f32) - jnp.where(rank <= room, 0.0, 1.0)
            bias = jnp.where(x >= 1.0, 0.0, NEG_INF)
            if masked:
                bias = jnp.where(_chunk_valid_t(first_q, k0, tq, tk, lk), bias, NEG_INF)
            bias_sc[...] = bias
            tie_room_sc[...] = room - rank[tk - 1:tk, :]
        s = buf[...] + bias_sc[...]
        _softmax_step_t(s, v_ref[0, h * dva:(h + 1) * dva, pl.ds(k0, tk)], h, m_sc, acc_sc)

    _score_pipeline(N_HEADS, n_kb, qk, sm, (sa_sc, sb_sc))
    _store_head_pairs_t(o_ref, [_normalized_t(acc_sc, h, HEAD_DIM) for h in range(N_HEADS)])


def _dsa_attention_t(q_t, k16, v_t, qi_t, ki2_16, wi_t, past_len, lk, tq, tk):
    b, width, lq = q_t.shape
    lkp = k16.shape[1]
    topk = min(IDX_TOPK_MAX, lk // 4)
    assert tk >= topk and lkp % tk == 0 and tk % tq == 0 and past_len % tk == 0 and tq % CHUNK == 0
    return pl.pallas_call(
        functools.partial(_dsa_t_kernel, tq=tq, tk=tk, past_len=past_len, lk=lk, topk=topk),
        grid=(b, lq // tq),
        in_specs=[
            pl.BlockSpec((1, width, tq), lambda bi, qi: (bi, 0, qi)),
            _resident_spec((1, lkp, width), lambda bi, qi: (bi, 0, 0)),
            _resident_spec((1, N_HEADS * (HEAD_DIM + VALUE_PAD), lkp), lambda bi, qi: (bi, 0, 0)),
            pl.BlockSpec((1, width, tq), lambda bi, qi: (bi, 0, qi)),
            _resident_spec((1, lkp, LANES), lambda bi, qi: (bi, 0, 0)),
            pl.BlockSpec((1, N_HEADS, tq), lambda bi, qi: (bi, 0, qi)),
        ],
        out_specs=pl.BlockSpec((1, tq, width), lambda bi, qi: (bi, qi, 0)),
        out_shape=jax.ShapeDtypeStruct((b, lq, width), f32),
        scratch_shapes=[
            pltpu.VMEM((lkp, tq), jnp.int16),
            pltpu.VMEM((lkp, tq), jnp.int16),
            pltpu.VMEM((tk, tq), f32),
            pltpu.VMEM((tk, tk), bf16),
            pltpu.VMEM((1, tq), f32),
            pltpu.VMEM((N_HEADS, 1, tq), f32),
            pltpu.VMEM((N_HEADS, HEAD_DIM + VALUE_PAD, tq), f32),
            pltpu.VMEM((tk, tq), f32),
            pltpu.VMEM((tk, tq), f32),
        ],
        compiler_params=_cparams(("parallel", "arbitrary")),
        name="dsa_attention_t",
    )(q_t, k16, v_t, qi_t, ki2_16, wi_t)


def _out_kernel(n_parts, final, *refs):
    parts = refs[:2 * n_parts]
    w_ref, x_ref = refs[2 * n_parts], refs[2 * n_parts + 1]
    rest = refs[2 * n_parts + 2:]
    acc = x_ref[...]
    row = 0
    for i in range(n_parts):
        o, g = parts[2 * i][...], parts[2 * i + 1][...]
        mixed = (o * (g / (1.0 + jnp.exp(-g)))).astype(bf16)
        width = mixed.shape[1]
        acc = acc + jnp.dot(mixed, w_ref[row:row + width, :], preferred_element_type=f32)
        row += width
    if final:
        fg_ref, xo_ref, y_ref = rest
        ms = jnp.mean(acc * acc, axis=-1, keepdims=True)
        y_ref[...] = acc * lax.rsqrt(ms + NORM_EPS) * fg_ref[...]
    else:
        (xo_ref,) = rest
    xo_ref[...] = acc


def _out_proj(parts, w16, x2d, final_gain, tm):
    n, d = x2d.shape
    n_parts = len(parts)
    final = final_gain is not None
    in_specs, args = [], []
    for (o, g) in parts:
        for a in (o, g):
            in_specs.append(pl.BlockSpec((tm, a.shape[1]), lambda i: (i, 0)))
            args.append(a)
    in_specs += [pl.BlockSpec(w16.shape, lambda i: (0, 0)), pl.BlockSpec((tm, d), lambda i: (i, 0))]
    args += [w16, x2d]
    out_shape = [jax.ShapeDtypeStruct((n, d), f32)]
    out_specs = [pl.BlockSpec((tm, d), lambda i: (i, 0))]
    if final:
        in_specs.append(pl.BlockSpec((1, d), lambda i: (0, 0)))
        args.append(final_gain)
        out_shape.append(jax.ShapeDtypeStruct((n, d), f32))
        out_specs.append(pl.BlockSpec((tm, d), lambda i: (i, 0)))
    res = pl.pallas_call(
        functools.partial(_out_kernel, n_parts, final),
        grid=(n // tm,),
        in_specs=in_specs,
        out_specs=out_specs,
        out_shape=out_shape,
        compiler_params=_cparams(("parallel",)),
        name="gated_out_proj",
    )(*args)
    return res if final else (res[0], None)


KEY_MAJOR_TILE = 512
FOX_HEADS_PER_STEP = 4
DIFF_HEADS_PER_STEP = 2


def _row_tile(n):
    return 256 if n % 256 == 0 else n


def _key_major(lq, lk, past_len):
    return lq % KEY_MAJOR_TILE == 0 and lk % KEY_MAJOR_TILE == 0 and past_len % KEY_MAJOR_TILE == 0


def _padded_keys(lq, lk, past_len):
    return lk if _key_major(lq, lk, past_len) else -(-lk // LANES) * LANES


def _pad_new_rows(x):
    return jnp.pad(x, ((0, 0), (0, NEW_PAD - x.shape[1]), (0, 0)))


def _even_layer(x, gain, w_in16, w_out16, b_forget, past_len, cache, final_gain):
    b, lq, d = x.shape
    n = b * lq
    lk = past_len + lq
    lkp = _padded_keys(lq, lk, past_len)
    tm = _row_tile(n)
    pos = past_len + jnp.arange(lq)
    cos, sa, sb = (jnp.tile(t, (b, 1)) for t in _rope_tables(pos))
    bias = jnp.zeros((1, LANES), f32).at[0, HEAD_DIM:HEAD_DIM + N_HEADS].set(b_forget.astype(f32))
    (aq16, ak, ak16, av, av16, ag, bq16, bk, bk16, bv, bv16, bg, qi16, misc) = _proj(
        x.reshape(n, d), gain.reshape(1, d), w_in16, cos, sa, sb, bias, EVEN_PLAN, tm)
    ki = misc[:, :HEAD_DIM]
    logf = misc[:, HEAD_DIM:HEAD_DIM + N_HEADS]
    wi = misc[:, HEAD_DIM + N_HEADS:HEAD_DIM + 2 * N_HEADS]

    def r3(a):
        return a.reshape(b, lq, a.shape[-1])

    ki16 = r3(ki).astype(bf16)
    ki2 = jnp.concatenate([ki16, ki16], axis=-1)
    logf3 = r3(logf)
    if cache is None:
        logf_all = logf3
    else:
        logf_all = jnp.pad(jnp.concatenate([cache[2].astype(f32), logf3], axis=1), ((0, 0), (0, lkp - lk), (0, 0)))
    cum = _cumsum_lanes(jnp.swapaxes(logf_all, 1, 2).reshape(b * N_HEADS, lkp)).reshape(b, N_HEADS, lkp)
    cum = cum * LOG2E

    if cache is None:
        assert _key_major(lq, lk, past_len)
        q_aug_t, k_aug, v_aug_t = _fox_operands_t(r3(aq16), r3(ak16), r3(av16), cum, past_len)
        a_out = _fox_attention_t(q_aug_t, k_aug, v_aug_t, past_len, KEY_MAJOR_TILE, KEY_MAJOR_TILE,
                                 FOX_HEADS_PER_STEP)
        b_out = _dsa_attention_t(jnp.swapaxes(r3(bq16), 1, 2), r3(bk16), _values_aug_t(r3(bv16), HEAD_DIM),
                                 jnp.swapaxes(r3(qi16), 1, 2), ki2, jnp.swapaxes(r3(wi), 1, 2),
                                 past_len, lk, KEY_MAJOR_TILE, KEY_MAJOR_TILE)
    else:
        c_ak4, c_av4, _, c_bk4, c_bv4, c_bki, layer = cache
        past = c_ak4.shape[2]
        assert past == past_len and lq <= NEW_PAD
        cum_new = cum[:, :, past:past + lq]
        ckn = jnp.pad(cum_new, ((0, 0), (0, 0), (0, NEW_PAD - lq)))
        a_out = _fox_attention_cached(r3(aq16), c_ak4, c_av4, layer, _pad_new_rows(r3(ak16)),
                                      _pad_new_rows(r3(av16)), jnp.swapaxes(cum_new, 1, 2), cum[:, :, :past], ckn)
        kic = c_bki.astype(bf16)
        b_out = _dsa_attention_cached(r3(bq16), c_bk4, c_bv4, layer, _pad_new_rows(r3(bk16)),
                                      _pad_new_rows(r3(bv16)), r3(qi16), jnp.concatenate([kic, kic], axis=-1),
                                      _pad_new_rows(ki2), r3(wi), past_len)
    x_new, y = _out_proj([(a_out.reshape(n, -1), ag), (b_out.reshape(n, -1), bg)], w_out16, x.reshape(n, d),
                         final_gain, tm)
    state = (ak.reshape(b, lq, N_HEADS, HEAD_DIM), av.reshape(b, lq, N_HEADS, HEAD_DIM), logf3,
             bk.reshape(b, lq, N_HEADS, HEAD_DIM), bv.reshape(b, lq, N_HEADS, HEAD_DIM), r3(ki))
    return x_new.reshape(b, lq, d), (None if y is None else y.reshape(b, lq, d)), state


def _odd_layer(x, gain, w_in16, lam_params, head_gain, w_out16, lam_init, past_len, cache, final_gain):
    b, lq, d = x.shape
    n = b * lq
    lk = past_len + lq
    tm = _row_tile(n)
    pos = past_len + jnp.arange(lq)
    cos, sa, sb = (jnp.tile(t, (b, 1)) for t in _rope_tables(pos))
    bias = jnp.zeros((1, LANES), f32)
    q16, k, k16, v, v16, g = _proj(x.reshape(n, d), gain.reshape(1, d), w_in16, cos, sa, sb, bias, ODD_PLAN, tm)

    def r3(a):
        return a.reshape(b, lq, a.shape[-1])

    lam_p, hg = lam_params.astype(f32), head_gain.reshape(1, LANES).astype(f32)
    if cache is None:
        assert _key_major(lq, lk, past_len)
        o = _diff_attention_t(jnp.swapaxes(r3(q16), 1, 2), r3(k16), _values_aug_t(r3(v16), LANES), lam_p, hg,
                              lam_init, past_len, lk, KEY_MAJOR_TILE, KEY_MAJOR_TILE, DIFF_HEADS_PER_STEP)
    else:
        c_ck4, c_cv4, layer = cache
        assert c_ck4.shape[2] == past_len and lq <= NEW_PAD
        o = _diff_attention_cached(r3(q16), c_ck4, c_cv4, layer, _pad_new_rows(r3(k16)), _pad_new_rows(r3(v16)),
                                   lam_p, hg, lam_init, past_len)
    x_new, y = _out_proj([(o.reshape(n, -1), g)], w_out16, x.reshape(n, d), final_gain, tm)
    state = (k.reshape(b, lq, N_HEADS, 2 * HEAD_DIM), v.reshape(b, lq, N_HEADS, 2 * HEAD_DIM))
    return x_new.reshape(b, lq, d), (None if y is None else y.reshape(b, lq, d)), state


def kernel(x_prompt, x_sample, cache_a_k, cache_a_v, cache_a_logf, cache_b_k, cache_b_v, cache_b_kidx, cache_c_k, cache_c_v, norm_gain, final_gain, w_in_even, b_forget, w_out_even, w_in_odd, lambda_params, c_head_gain, w_out_odd):
    past_len = cache_a_k.shape[2]
    depth = norm_gain.shape[0]
    xp, xs = x_prompt, x_sample
    yp = ys = None
    even_p, even_s, odd_p, odd_s = [], [], [], []
    fg = final_gain.reshape(1, -1).astype(f32)

    def rows4(c):
        return c.reshape(c.shape[0], c.shape[1], c.shape[2], -1)

    ca_k, ca_v, cb_k, cb_v, cc_k, cc_v = (rows4(c) for c in (cache_a_k, cache_a_v, cache_b_k, cache_b_v,
                                                            cache_c_k, cache_c_v))
    for layer in range(depth):
        g = norm_gain[layer]
        i = layer // 2
        last = fg if layer == depth - 1 else None
        if layer % 2 == 0:
            w_in16 = _reorder_even_weight(w_in_even[i])
            w_out16 = w_out_even[i].astype(bf16)
            xp, yp, st_p = _even_layer(xp, g, w_in16, w_out16, b_forget[i], 0, None, last)
            xs, ys, st_s = _even_layer(xs, g, w_in16, w_out16, b_forget[i], past_len,
                                       (ca_k, ca_v, cache_a_logf[i], cb_k, cb_v, cache_b_kidx[i], i), last)
            even_p.append(st_p)
            even_s.append(st_s)
        else:
            lam_init = 0.8 - 0.6 * math.exp(-0.3 * layer)
            w_in16 = w_in_odd[i].astype(bf16)
            w_out16 = w_out_odd[i].astype(bf16)
            xp, yp, st_p = _odd_layer(xp, g, w_in16, lambda_params[i], c_head_gain[i], w_out16, lam_init,
                                      0, None, last)
            xs, ys, st_s = _odd_layer(xs, g, w_in16, lambda_params[i], c_head_gain[i], w_out16, lam_init,
                                      past_len, (cc_k, cc_v, i), last)
            odd_p.append(st_p)
            odd_s.append(st_s)
    a_k_p, a_v_p, a_f_p, b_k_p, b_v_p, b_i_p = [jnp.stack(t) for t in zip(*even_p)]
    a_k_s, a_v_s, a_f_s, b_k_s, b_v_s, b_i_s = [jnp.stack(t) for t in zip(*even_s)]
    c_k_p, c_v_p = [jnp.stack(t) for t in zip(*odd_p)]
    c_k_s, c_v_s = [jnp.stack(t) for t in zip(*odd_s)]
    return (yp, ys,
            a_k_p, a_v_p, a_f_p, b_k_p, b_v_p, b_i_p, c_k_p, c_v_p,
            a_k_s, a_v_s, a_f_s, b_k_s, b_v_s, b_i_s, c_k_s, c_v_s)
```
